```python
import math
import jax, jax.numpy as jnp
from jax import lax
import numpy as np

D_MODEL = 1024
BATCH = 4
SEQ = 8192
DEPTH = 2

HEAD_DIM = 64
N_HEADS_A = 8
N_HEADS_B = 8
D_ATTN = (N_HEADS_A + N_HEADS_B) * HEAD_DIM
D_ATTN_IN = 3 * D_ATTN + N_HEADS_B
MOBA_BLOCK = 256
MOBA_TOPK = 3
MOBA_QCHUNK = 32
FOX_QBLOCK = 128
REL_BUCKETS = 32
REL_MAX_DIST = 128
SGU_CHUNK = 128
SGU_GROUPS = 8
SGU_WIDTH = D_MODEL
D_FF = 2816
PLE_DIM = 256
N_EVEN = (DEPTH + 1) // 2
N_ODD = DEPTH // 2
DN_ALPHA = (2.0 * DEPTH) ** 0.25
DN_BETA = (8.0 * DEPTH) ** -0.25
LN_EPS = 1e-5
NEG_INF = -1e30

kernel_name = "hybrid_moba_fox_gmlp_macaron_deepnorm"


def layer_norm(x, g, b):
    xf = x.astype(jnp.float32)
    mu = jnp.mean(xf, axis=-1, keepdims=True)
    var = jnp.mean(jnp.square(xf - mu), axis=-1, keepdims=True)
    y = (xf - mu) * lax.rsqrt(var + LN_EPS)
    return (y * g.astype(jnp.float32) + b.astype(jnp.float32)).astype(x.dtype)


def swiglu(x, w_gate, w_up, w_down):
    return (jax.nn.silu(x @ w_gate) * (x @ w_up)) @ w_down


def t5_bucket(n):
    n = jnp.maximum(n, 0)
    max_exact = REL_BUCKETS // 2
    nf = jnp.maximum(n, max_exact).astype(jnp.float32)
    large = max_exact + (jnp.log(nf / max_exact) / math.log(REL_MAX_DIST / max_exact)
                         * (REL_BUCKETS - max_exact)).astype(jnp.int32)
    large = jnp.minimum(large, REL_BUCKETS - 1)
    return jnp.where(n < max_exact, n, large)


def moba_attention(q, k, v, rel_bias):
    B, H, S, hd = q.shape
    nb = -(-S // MOBA_BLOCK)
    sp = nb * MOBA_BLOCK
    pad = ((0, 0), (0, 0), (0, sp - S), (0, 0))
    q, k, v = jnp.pad(q, pad), jnp.pad(k, pad), jnp.pad(v, pad)
    kb = k.reshape(B, H, nb, MOBA_BLOCK, hd)
    vb = v.reshape(B, H, nb, MOBA_BLOCK, hd)
    kbar = jnp.mean(kb.astype(jnp.float32), axis=3).astype(k.dtype)
    topk = min(MOBA_TOPK, nb)
    tbl = rel_bias.T.astype(jnp.float32)
    bi = jnp.arange(B)[:, None, None, None]
    hi = jnp.arange(H)[None, :, None, None]
    r = jnp.arange(MOBA_BLOCK)
    scale = hd ** -0.5
    nsel = topk * MOBA_BLOCK

    def chunk(c):
        start = c * MOBA_QCHUNK
        blk = start // MOBA_BLOCK
        qc = lax.dynamic_slice_in_dim(q, start, MOBA_QCHUNK, axis=2)
        t = start + jnp.arange(MOBA_QCHUNK)
        gate = jnp.einsum('bhqd,bhnd->bhqn', qc, kbar).astype(jnp.float32)
        gate = jnp.where(jnp.arange(nb) < blk, gate, NEG_INF)
        _, idx = lax.top_k(gate, topk)
        valid = jnp.arange(topk) < blk
        kg = kb[bi, hi, idx]
        vg = vb[bi, hi, idx]
        s_sel = jnp.einsum('bhqd,bhqkrd->bhqkr', qc, kg).astype(jnp.float32) * scale
        pos_sel = idx[..., None] * MOBA_BLOCK + r
        s_sel = s_sel + tbl[hi[..., None], t5_bucket(t[:, None, None] - pos_sel)]
        s_sel = jnp.where(valid[:, None], s_sel, NEG_INF)
        k_own = lax.dynamic_slice_in_dim(kb, blk, 1, axis=2)[:, :, 0]
        v_own = lax.dynamic_slice_in_dim(vb, blk, 1, axis=2)[:, :, 0]
        rel_own = t[:, None] - (blk * MOBA_BLOCK + r)[None, :]
        s_own = (jnp.einsum('bhqd,bhrd->bhqr', qc, k_own).astype(jnp.float32) * scale
                 + tbl[:, t5_bucket(rel_own)])
        s_own = jnp.where(rel_own >= 0, s_own, NEG_INF)
        logits = jnp.concatenate([s_sel.reshape(B, H, MOBA_QCHUNK, nsel), s_own], axis=-1)
        prob = jax.nn.softmax(logits, axis=-1).astype(v.dtype)
        p_sel = prob[..., :nsel].reshape(B, H, MOBA_QCHUNK, topk, MOBA_BLOCK)
        return (jnp.einsum('bhqkr,bhqkrd->bhqd', p_sel, vg)
                + jnp.einsum('bhqr,bhrd->bhqd', prob[..., nsel:], v_own))

    out = lax.map(chunk, jnp.arange(sp // MOBA_QCHUNK))
    return jnp.moveaxis(out, 0, 2).reshape(B, H, sp, hd)[:, :, :S]


def forgetting_attention(q, k, v, log_f):
    B, H, S, hd = q.shape
    c = jnp.cumsum(log_f, axis=-1)
    scale = hd ** -0.5
    s_pos = jnp.arange(S)

    def block(i):
        start = i * FOX_QBLOCK
        qb = lax.dynamic_slice_in_dim(q, start, FOX_QBLOCK, axis=2)
        cb = lax.dynamic_slice_in_dim(c, start, FOX_QBLOCK, axis=2)
        t = start + jnp.arange(FOX_QBLOCK)
        logits = (jnp.einsum('bhqd,bhsd->bhqs', qb, k).astype(jnp.float32) * scale
                  + (cb[..., None] - c[:, :, None, :]))
        logits = jnp.where(t[:, None] >= s_pos[None, :], logits, NEG_INF)
        prob = jax.nn.softmax(logits, axis=-1).astype(v.dtype)
        return jnp.einsum('bhqs,bhsd->bhqd', prob, v)

    out = lax.map(block, jnp.arange(S // FOX_QBLOCK))
    return jnp.moveaxis(out, 0, 2).reshape(B, H, S, hd)


def mixer_attention(x, w_in, b_f, w_out, rel_bias):
    B, S, _ = x.shape
    h = x @ w_in
    wa = N_HEADS_A * HEAD_DIM
    wb = N_HEADS_B * HEAD_DIM
    qkv_a = h[..., :3 * wa].reshape(B, S, 3, N_HEADS_A, HEAD_DIM).transpose(2, 0, 3, 1, 4)
    qkv_b = h[..., 3 * wa:3 * wa + 3 * wb].reshape(B, S, 3, N_HEADS_B, HEAD_DIM).transpose(2, 0, 3, 1, 4)
    f_logit = (h[..., 3 * wa + 3 * wb:] + b_f).astype(jnp.float32)
    log_f = jax.nn.log_sigmoid(f_logit).transpose(0, 2, 1)
    o_a = moba_attention(qkv_a[0], qkv_a[1], qkv_a[2], rel_bias)
    o_b = forgetting_attention(qkv_b[0], qkv_b[1], qkv_b[2], log_f)
    o = jnp.concatenate([o_a, o_b], axis=1).transpose(0, 2, 1, 3).reshape(B, S, D_ATTN)
    return o @ w_out


def mixer_sgu(x, w_in, b_in, ln_g, ln_b, w_s, b_s, w_out):
    B, S, _ = x.shape
    h = jax.nn.gelu(x @ w_in + b_in)
    u, v = jnp.split(h, 2, axis=-1)
    v = layer_norm(v, ln_g, ln_b)
    nc = S // SGU_CHUNK
    gw = SGU_WIDTH // SGU_GROUPS
    v = v.reshape(B, nc, SGU_CHUNK, SGU_GROUPS, gw)
    causal = jnp.tril(jnp.ones((SGU_CHUNK, SGU_CHUNK), dtype=bool))
    w = jnp.where(causal, w_s, 0)
    mixed = jnp.einsum('gts,bcsgd->bctgd', w, v) + b_s.T[None, None, :, :, None]
    y = u * mixed.reshape(B, S, SGU_WIDTH)
    return y @ w_out


def setup_inputs(seed: int = 0) -> dict:
    key = jax.random.key(seed)
    ks = jax.random.split(key, 20)
    D = D_MODEL

    def nrm(k, shape, s):
        return jax.random.normal(k, shape, jnp.float32) * s

    return {
        "x": nrm(ks[0], (BATCH, SEQ, D), 1.0),
        "p": nrm(ks[1], (DEPTH, BATCH, SEQ, PLE_DIM), 1.0),
        "ln_g": 1.0 + nrm(ks[2], (DEPTH, 3, D), 0.02),
        "ln_b": nrm(ks[3], (DEPTH, 3, D), 0.02),
        "ffn_w_gate": nrm(ks[4], (DEPTH, 2, D, D_FF), D ** -0.5),
        "ffn_w_up": nrm(ks[5], (DEPTH, 2, D, D_FF), D ** -0.5),
        "ffn_w_down": nrm(ks[6], (DEPTH, 2, D_FF, D), DN_BETA * D_FF ** -0.5),
        "attn_w_in": nrm(ks[7], (N_EVEN, D, D_ATTN_IN), D ** -0.5),
        "attn_b_f": jax.random.uniform(ks[8], (N_EVEN, N_HEADS_B), jnp.float32, 2.0, 6.0),
        "attn_w_out": nrm(ks[9], (N_EVEN, D_ATTN, D), DN_BETA * D_ATTN ** -0.5),
        "rel_bias": nrm(ks[10], (REL_BUCKETS, N_HEADS_A), 0.5),
        "sgu_w_in": nrm(ks[11], (N_ODD, D, 2 * SGU_WIDTH), D ** -0.5),
        "sgu_b_in": nrm(ks[12], (N_ODD, 2 * SGU_WIDTH), 0.02),
        "sgu_ln_g": 1.0 + nrm(ks[13], (N_ODD, SGU_WIDTH), 0.02),
        "sgu_ln_b": nrm(ks[14], (N_ODD, SGU_WIDTH), 0.02),
        "sgu_w_s": nrm(ks[15], (N_ODD, SGU_GROUPS, SGU_CHUNK, SGU_CHUNK), SGU_CHUNK ** -0.5),
        "sgu_b_s": 1.0 + nrm(ks[16], (N_ODD, SGU_GROUPS, SGU_CHUNK), 0.1),
        "sgu_w_out": nrm(ks[17], (N_ODD, SGU_WIDTH, D), DN_BETA * SGU_WIDTH ** -0.5),
        "ple_w_proj": nrm(ks[18], (DEPTH, PLE_DIM, D), PLE_DIM ** -0.5),
        "ple_w_gate": nrm(ks[19], (DEPTH, D, D), D ** -0.5),
    }


def reference(x, p, ln_g, ln_b, ffn_w_gate, ffn_w_up, ffn_w_down, attn_w_in, attn_b_f,
              attn_w_out, rel_bias, sgu_w_in, sgu_b_in, sgu_ln_g, sgu_ln_b, sgu_w_s, sgu_b_s,
              sgu_w_out, ple_w_proj, ple_w_gate):
    for i in range(DEPTH):
        j = i // 2
        x = layer_norm(DN_ALPHA * x + 0.5 * swiglu(x, ffn_w_gate[i, 0], ffn_w_up[i, 0], ffn_w_down[i, 0]),
                       ln_g[i, 0], ln_b[i, 0])
        if i % 2 == 0:
            m = mixer_attention(x, attn_w_in[j], attn_b_f[j], attn_w_out[j], rel_bias)
        else:
            m = mixer_sgu(x, sgu_w_in[j], sgu_b_in[j], sgu_ln_g[j], sgu_ln_b[j],
                          sgu_w_s[j], sgu_b_s[j], sgu_w_out[j])
        x = layer_norm(DN_ALPHA * x + m, ln_g[i, 1], ln_b[i, 1])
        x = layer_norm(DN_ALPHA * x + 0.5 * swiglu(x, ffn_w_gate[i, 1], ffn_w_up[i, 1], ffn_w_down[i, 1]),
                       ln_g[i, 2], ln_b[i, 2])
        x = x + jax.nn.sigmoid(x @ ple_w_gate[i]) * (p[i] @ ple_w_proj[i])
    return x
```

```python
import functools
import math

import jax
import jax.numpy as jnp
from jax import lax
from jax.experimental import pallas as pl
from jax.experimental.pallas import tpu as pltpu

D_MODEL = 1024
HEAD_DIM = 64
N_HEADS_A = 8
N_HEADS_B = 8
W_HEADS = N_HEADS_A * HEAD_DIM
MOBA_BLOCK = 256
MOBA_TOPK = 3
REL_BUCKETS = 32
REL_MAX_DIST = 128
SGU_CHUNK = 128
SGU_GROUPS = 8
D_FF = 2816
PLE_DIM = 256
DEPTH = 2
DN_ALPHA = (2.0 * DEPTH) ** 0.25
LN_EPS = 1e-5
NEG_INF = -1e30

F32 = jnp.float32
BF16 = jnp.bfloat16

VMEM_LIMIT_BYTES = 56 * 1024 * 1024
LANES = 128
HEADS_PER_TILE = LANES // HEAD_DIM
FF_CHUNK = 256
ROW_TILE = 512
ATTN_TILE = MOBA_BLOCK


def _params(*semantics):
    return pltpu.CompilerParams(dimension_semantics=semantics, vmem_limit_bytes=VMEM_LIMIT_BYTES)


def _resident(shape):
    zeros = (0,) * len(shape)
    return pl.BlockSpec(shape, lambda *_: zeros, pipeline_mode=pl.Buffered(1))


def _layer_norm(y, g, b):
    mu = jnp.mean(y, axis=-1, keepdims=True)
    d = y - mu
    var = jnp.mean(d * d, axis=-1, keepdims=True)
    return d * lax.rsqrt(var + LN_EPS) * g + b


def _dot(a, b):
    return jnp.dot(a, b, preferred_element_type=F32)


def _ffn_body(x_ref, wg_ref, wu_ref, wd_ref, g_ref, b_ref, *rest, with_ple):
    if with_ple:
        p_ref, wpg_ref, wpp_ref, o_ref = rest
    else:
        (o_ref,) = rest
    x = x_ref[...]
    xb = x.astype(BF16)
    acc = jnp.zeros(x.shape, F32)
    for c in range(D_FF // FF_CHUNK):
        sl = slice(c * FF_CHUNK, (c + 1) * FF_CHUNK)
        gate = _dot(xb, wg_ref[:, sl])
        up = _dot(xb, wu_ref[:, sl])
        h = (jax.nn.silu(gate) * up).astype(BF16)
        acc = acc + _dot(h, wd_ref[sl, :])
    y = _layer_norm(DN_ALPHA * x + 0.5 * acc, g_ref[...], b_ref[...])
    if with_ple:
        gate = jax.nn.sigmoid(_dot(y.astype(BF16), wpg_ref[...]))
        y = y + gate * _dot(p_ref[...].astype(BF16), wpp_ref[...])
    o_ref[...] = y


def _ffn(x, wg, wu, wd, g, b, ple=None):
    m, d = x.shape
    tm = min(ROW_TILE, m)
    row = lambda w: pl.BlockSpec((tm, w), lambda i: (i, 0))
    in_specs = [row(d), _resident(wg.shape), _resident(wu.shape), _resident(wd.shape),
                _resident(g.shape), _resident(b.shape)]
    args = [x, wg, wu, wd, g, b]
    if ple is not None:
        p, wpg, wpp = ple
        in_specs += [row(p.shape[1]), _resident(wpg.shape), _resident(wpp.shape)]
        args += [p, wpg, wpp]
    return pl.pallas_call(
        functools.partial(_ffn_body, with_ple=ple is not None),
        grid=(m // tm,),
        in_specs=in_specs,
        out_specs=row(d),
        out_shape=jax.ShapeDtypeStruct((m, d), F32),
        compiler_params=_params("parallel"),
        name="ffn_ple" if ple is not None else "ffn",
    )(*args)


def _proj_body(x_ref, w_ref, wt_ref, qa_ref, ka_ref, qb_ref, kb_ref, vta_ref, vtb_ref, f_ref):
    xb = x_ref[0].astype(BF16)
    scale = HEAD_DIM ** -0.5

    def seg(n):
        return _dot(xb, w_ref[:, n * W_HEADS:(n + 1) * W_HEADS])

    qa_ref[0] = (seg(0) * scale).astype(BF16)
    ka_ref[0] = seg(1).astype(BF16)
    qb_ref[0] = (seg(2) * scale).astype(BF16)
    kb_ref[0] = seg(3).astype(BF16)
    rt = lax.dot_general(wt_ref[...], xb, (((1,), (1,)), ((), ())), preferred_element_type=F32)
    for n in range(vta_ref.shape[1]):
        cols = slice(n * ATTN_TILE, (n + 1) * ATTN_TILE)
        vta_ref[0, n] = rt[:W_HEADS, cols].astype(BF16)
        vtb_ref[0, n] = rt[W_HEADS:2 * W_HEADS, cols].astype(BF16)
    f_ref[0] = rt[2 * W_HEADS:, :]


def _attn_proj(x, w_rows, w_t):
    bsz, s, d = x.shape
    tm = min(ROW_TILE, s)
    n_blk = tm // ATTN_TILE
    rows = pl.BlockSpec((1, tm, W_HEADS), lambda b, i: (b, i, 0))
    vts = pl.BlockSpec((1, n_blk, W_HEADS, ATTN_TILE), lambda b, i: (b, i, 0, 0))
    row_t = jax.ShapeDtypeStruct((bsz, s, W_HEADS), BF16)
    vt_t = jax.ShapeDtypeStruct((bsz, s // ATTN_TILE, W_HEADS, ATTN_TILE), BF16)
    return pl.pallas_call(
        _proj_body,
        grid=(bsz, s // tm),
        in_specs=[pl.BlockSpec((1, tm, d), lambda b, i: (b, i, 0)),
                  _resident(w_rows.shape), _resident(w_t.shape)],
        out_specs=[rows, rows, rows, rows, vts, vts,
                   pl.BlockSpec((1, N_HEADS_B, tm), lambda b, i: (b, 0, i))],
        out_shape=[row_t, row_t, row_t, row_t, vt_t, vt_t,
                   jax.ShapeDtypeStruct((bsz, N_HEADS_B, s), F32)],
        compiler_params=_params("parallel", "parallel"),
        name="attn_proj",
    )(x, w_rows, w_t)


def _decay_body(f_ref, bf_ref, c_ref):
    x = f_ref[0] + bf_ref[...]
    log_f = jnp.minimum(x, 0.0) - jnp.log(1.0 + jnp.exp(-jnp.abs(x)))
    r = lax.broadcasted_iota(jnp.int32, (LANES, LANES), 0)
    c = lax.broadcasted_iota(jnp.int32, (LANES, LANES), 1)
    prefix = (r <= c).astype(F32)
    carry = jnp.zeros((x.shape[0], 1), F32)
    for n in range(x.shape[1] // LANES):
        sl = slice(n * LANES, (n + 1) * LANES)
        cs = jnp.dot(log_f[:, sl], prefix, preferred_element_type=F32,
                     precision=lax.Precision.HIGHEST) + carry
        c_ref[0, :, sl] = cs
        carry = cs[:, LANES - 1:LANES]


def _fox_decay(f_t, b_f):
    bsz, h, s = f_t.shape
    return pl.pallas_call(
        _decay_body,
        grid=(bsz,),
        in_specs=[pl.BlockSpec((1, h, s), lambda b: (b, 0, 0)), _resident((h, 1))],
        out_specs=pl.BlockSpec((1, h, s), lambda b: (b, 0, 0)),
        out_shape=jax.ShapeDtypeStruct((bsz, h, s), F32),
        compiler_params=_params("parallel"),
        name="fox_decay",
    )(f_t, b_f.reshape(h, 1))


def _bias_body(tbl_ref, o_ref):
    h = pl.program_id(0)
    r = lax.broadcasted_iota(jnp.int32, (MOBA_BLOCK, MOBA_BLOCK), 0)
    t = lax.broadcasted_iota(jnp.int32, (MOBA_BLOCK, MOBA_BLOCK), 1)
    max_exact = REL_BUCKETS // 2
    for slot in range(3):
        n = jnp.maximum(t - r + slot * MOBA_BLOCK, 0)
        nf = jnp.maximum(n, max_exact).astype(F32)
        large = max_exact + (jnp.log(nf / max_exact) / math.log(REL_MAX_DIST / max_exact)
                             * (REL_BUCKETS - max_exact)).astype(jnp.int32)
        large = jnp.minimum(large, REL_BUCKETS - 1)
        bucket = jnp.where(n < max_exact, n, large)
        bias = jnp.zeros((MOBA_BLOCK, MOBA_BLOCK), F32)
        for k in range(REL_BUCKETS):
            bias = jnp.where(bucket == k, tbl_ref[k, h], bias)
        o_ref[0, slot] = bias


def _moba_bias(rel_bias):
    return pl.pallas_call(
        _bias_body,
        grid=(N_HEADS_A,),
        in_specs=[pl.BlockSpec(memory_space=pltpu.SMEM)],
        out_specs=pl.BlockSpec((1, 3, MOBA_BLOCK, MOBA_BLOCK), lambda h: (h, 0, 0, 0)),
        out_shape=jax.ShapeDtypeStruct((N_HEADS_A, 3, MOBA_BLOCK, MOBA_BLOCK), F32),
        compiler_params=_params("parallel"),
        name="moba_bias",
    )(rel_bias)


def _head_mask(shape, head):
    lane = lax.broadcasted_iota(jnp.int32, shape, len(shape) - 1)
    return (lane // HEAD_DIM) == head


def _causal_tile():
    r = lax.broadcasted_iota(jnp.int32, (ATTN_TILE, ATTN_TILE), 0)
    t = lax.broadcasted_iota(jnp.int32, (ATTN_TILE, ATTN_TILE), 1)
    return t >= r


def _scores(k_ref, qm, j):
    k = k_ref[0, pl.ds(pl.multiple_of(j * ATTN_TILE, ATTN_TILE), ATTN_TILE), :]
    return lax.dot_general(k, qm, (((1,), (1,)), ((), ())), preferred_element_type=F32)


def _online_update(s, vt, m, l, acc):
    m_new = jnp.maximum(m, jnp.max(s, axis=0, keepdims=True))
    p = jnp.exp(s - m_new)
    alpha = jnp.exp(m - m_new)
    l = alpha * l + jnp.sum(p, axis=0, keepdims=True)
    acc = alpha * acc + _dot(vt, p.astype(BF16))
    return m_new, l, acc


def _init_stats():
    return (jnp.full((1, ATTN_TILE), NEG_INF, F32), jnp.zeros((1, ATTN_TILE), F32),
            jnp.zeros((HEAD_DIM, ATTN_TILE), F32))


def _store_heads(o_ref, outs):
    o_ref[0] = jnp.concatenate(outs, axis=0).T.astype(o_ref.dtype)


def _fox_body(q_ref, k_ref, vt_ref, c_ref, o_ref):
    i = pl.program_id(2)
    q = q_ref[0]
    causal = _causal_tile()
    outs = []
    for h in range(HEADS_PER_TILE):
        rows = slice(h * HEAD_DIM, (h + 1) * HEAD_DIM)
        qm = jnp.where(_head_mask(q.shape, h), q, jnp.zeros_like(q))

        def decayed(j):
            c = c_ref[0, 0, pl.ds(pl.multiple_of(j * ATTN_TILE, ATTN_TILE), ATTN_TILE), h:h + 1]
            return _scores(k_ref, qm, j) - c

        s = jnp.where(causal, decayed(i), NEG_INF)
        carry = _online_update(s, vt_ref[0, i, rows, :], *_init_stats())

        def past(j, carry):
            return _online_update(decayed(j), vt_ref[0, j, rows, :], *carry)

        m, l, acc = lax.fori_loop(0, i, past, carry)
        outs.append(acc / l)
    _store_heads(o_ref, outs)


def _moba_body(q_ref, k_ref, vt_ref, bias_ref, o_ref, kbar_ref, sel_ref):
    i = pl.program_id(2)
    n_blk = kbar_ref.shape[0]

    @pl.when(i == 0)
    def _():
        for n in range(n_blk):
            kb = k_ref[0, n * MOBA_BLOCK:(n + 1) * MOBA_BLOCK, :].astype(F32)
            kbar_ref[n:n + 1, :] = jnp.mean(kb, axis=0, keepdims=True)

    q = q_ref[0]
    qf = q.astype(F32)
    kbar = kbar_ref[...]
    blk = lax.broadcasted_iota(jnp.int32, (n_blk, ATTN_TILE), 0)
    causal = _causal_tile()
    outs = []
    for h in range(HEADS_PER_TILE):
        rows = slice(h * HEAD_DIM, (h + 1) * HEAD_DIM)
        gate = lax.dot_general(jnp.where(_head_mask(kbar.shape, h), kbar, 0.0), qf,
                               (((1,), (1,)), ((), ())), preferred_element_type=F32,
                               precision=lax.Precision.HIGHEST)
        gate = jnp.where(blk < i, gate, NEG_INF)
        rank = jnp.zeros(gate.shape, jnp.int32)
        for n in range(n_blk):
            g_n = gate[n:n + 1, :]
            ahead = (g_n > gate) | ((g_n == gate) & (n < blk))
            rank = rank + ahead.astype(jnp.int32)
        sel_ref[h] = ((blk < i) & (rank < MOBA_TOPK)).astype(F32)

        qm = jnp.where(_head_mask(q.shape, h), q, jnp.zeros_like(q))
        s = jnp.where(causal, _scores(k_ref, qm, i) + bias_ref[h, 0], NEG_INF)
        carry = _online_update(s, vt_ref[0, i, rows, :], *_init_stats())

        def past(j, carry):
            slot = jnp.where(j == i - 1, 1, 2)
            s = _scores(k_ref, qm, j) + bias_ref[h, pl.ds(slot, 1)][0]
            s = jnp.where(sel_ref[h, pl.ds(j, 1), :] > 0.5, s, NEG_INF)
            return _online_update(s, vt_ref[0, j, rows, :], *carry)

        m, l, acc = lax.fori_loop(0, i, past, carry)
        outs.append(acc / l)
    _store_heads(o_ref, outs)


def _attn_specs(bsz, s):
    n_blk = s // ATTN_TILE
    q_spec = pl.BlockSpec((1, ATTN_TILE, LANES), lambda b, p, i: (b, i, p))
    k_spec = pl.BlockSpec((1, s, LANES), lambda b, p, i: (b, 0, p))
    vt_spec = pl.BlockSpec((1, n_blk, LANES, ATTN_TILE), lambda b, p, i: (b, 0, p, 0))
    out_shape = jax.ShapeDtypeStruct((bsz, s, W_HEADS), BF16)
    grid = (bsz, W_HEADS // LANES, n_blk)
    return grid, q_spec, k_spec, vt_spec, out_shape


def _fox_attention(q, k, vt, c_cols):
    bsz, s, _ = q.shape
    grid, q_spec, k_spec, vt_spec, out_shape = _attn_specs(bsz, s)
    return pl.pallas_call(
        _fox_body,
        grid=grid,
        in_specs=[q_spec, k_spec, vt_spec,
                  pl.BlockSpec((1, 1, s, HEADS_PER_TILE), lambda b, p, i: (b, p, 0, 0))],
        out_specs=q_spec,
        out_shape=out_shape,
        compiler_params=_params("parallel", "parallel", "arbitrary"),
        name="fox_attention",
    )(q, k, vt, c_cols)


def _moba_attention(q, k, vt, bias):
    bsz, s, _ = q.shape
    grid, q_spec, k_spec, vt_spec, out_shape = _attn_specs(bsz, s)
    n_blk = s // MOBA_BLOCK
    return pl.pallas_call(
        _moba_body,
        grid=grid,
        in_specs=[q_spec, k_spec, vt_spec,
                  pl.BlockSpec((HEADS_PER_TILE, 3, MOBA_BLOCK, MOBA_BLOCK),
                               lambda b, p, i: (p, 0, 0, 0))],
        out_specs=q_spec,
        out_shape=out_shape,
        scratch_shapes=[pltpu.VMEM((n_blk, LANES), F32),
                        pltpu.VMEM((HEADS_PER_TILE, n_blk, ATTN_TILE), F32)],
        compiler_params=_params("parallel", "parallel", "arbitrary"),
        name="moba_attention",
    )(q, k, vt, bias)


def _attn_out_body(x_ref, oa_ref, ob_ref, wa_ref, wb_ref, g_ref, b_ref, o_ref):
    mixed = _dot(oa_ref[...], wa_ref[...]) + _dot(ob_ref[...], wb_ref[...])
    o_ref[...] = _layer_norm(DN_ALPHA * x_ref[...] + mixed, g_ref[...], b_ref[...])


def _attn_out(x, o_a, o_b, w_a, w_b, g, b):
    m, d = x.shape
    tm = min(ROW_TILE, m)
    row = lambda w: pl.BlockSpec((tm, w), lambda i: (i, 0))
    return pl.pallas_call(
        _attn_out_body,
        grid=(m // tm,),
        in_specs=[row(d), row(W_HEADS), row(W_HEADS), _resident(w_a.shape), _resident(w_b.shape),
                  _resident(g.shape), _resident(b.shape)],
        out_specs=row(d),
        out_shape=jax.ShapeDtypeStruct((m, d), F32),
        compiler_params=_params("parallel"),
        name="attn_out",
    )(x, o_a, o_b, w_a, w_b, g, b)


def _sgu_body(x_ref, win_ref, bin_ref, lg_ref, lb_ref, ws_ref, bs_ref, wout_ref, g_ref, b_ref,
              o_ref, y_ref):
    x = x_ref[...]
    xb = x.astype(BF16)
    w = x.shape[1]
    u = jax.nn.gelu(_dot(xb, win_ref[:, :w]) + bin_ref[:, :w])
    v = jax.nn.gelu(_dot(xb, win_ref[:, w:]) + bin_ref[:, w:])
    vb = _layer_norm(v, lg_ref[...], lb_ref[...]).astype(BF16)
    t = lax.broadcasted_iota(jnp.int32, (SGU_CHUNK, SGU_CHUNK), 0)
    s = lax.broadcasted_iota(jnp.int32, (SGU_CHUNK, SGU_CHUNK), 1)
    gw = w // SGU_GROUPS
    for g in range(SGU_GROUPS):
        w_g = jnp.where(t >= s, ws_ref[g], 0.0).astype(BF16)
        cols = slice(g * gw, (g + 1) * gw)
        for c in range(x.shape[0] // SGU_CHUNK):
            rows = slice(c * SGU_CHUNK, (c + 1) * SGU_CHUNK)
            mixed = _dot(w_g, vb[rows, cols]) + bs_ref[:, g:g + 1]
            y_ref[rows, cols] = (u[rows, cols] * mixed).astype(BF16)
    out = _dot(y_ref[...], wout_ref[...])
    o_ref[...] = _layer_norm(DN_ALPHA * x + out, g_ref[...], b_ref[...])


def _sgu(x, w_in, b_in, ln_g, ln_b, w_s, b_s_t, w_out, g, b):
    m, d = x.shape
    tm = min(ROW_TILE, m)
    row = pl.BlockSpec((tm, d), lambda i: (i, 0))
    weights = [w_in, b_in, ln_g, ln_b, w_s, b_s_t, w_out, g, b]
    return pl.pallas_call(
        _sgu_body,
        grid=(m // tm,),
        in_specs=[row] + [_resident(a.shape) for a in weights],
        out_specs=row,
        out_shape=jax.ShapeDtypeStruct((m, d), F32),
        scratch_shapes=[pltpu.VMEM((tm, d), BF16)],
        compiler_params=_params("parallel"),
        name="sgu",
    )(x, *weights)


def _mixer_attention(x2, bsz, s, w_in, b_f, w_out, rel_bias, g, b):
    d = x2.shape[1]
    seg = lambda n: w_in[:, n * W_HEADS:(n + 1) * W_HEADS]
    w_rows = jnp.concatenate([seg(0), seg(1), seg(3), seg(4)], axis=1).astype(BF16)
    w_t = jnp.concatenate([seg(2), seg(5), w_in[:, 6 * W_HEADS:]], axis=1).T.astype(BF16)
    qa, ka, qb, kb, vta, vtb, f_t = _attn_proj(x2.reshape(bsz, s, d), w_rows, w_t)
    c = _fox_decay(f_t, b_f)
    c_cols = c.reshape(bsz, N_HEADS_B // HEADS_PER_TILE, HEADS_PER_TILE, s).transpose(0, 1, 3, 2)
    o_a = _moba_attention(qa, ka, vta, _moba_bias(rel_bias))
    o_b = _fox_attention(qb, kb, vtb, c_cols)
    w_out = w_out.astype(BF16)
    return _attn_out(x2, o_a.reshape(bsz * s, W_HEADS), o_b.reshape(bsz * s, W_HEADS),
                     w_out[:W_HEADS], w_out[W_HEADS:], g, b)


def kernel(x, p, ln_g, ln_b, ffn_w_gate, ffn_w_up, ffn_w_down, attn_w_in, attn_b_f, attn_w_out,
           rel_bias, sgu_w_in, sgu_b_in, sgu_ln_g, sgu_ln_b, sgu_w_s, sgu_b_s, sgu_w_out,
           ple_w_proj, ple_w_gate):
    bsz, s, d = x.shape
    x2 = x.reshape(bsz * s, d)
    vec = lambda a: a.reshape(1, -1)
    for i in range(DEPTH):
        j = i // 2
        ln = lambda n: (vec(ln_g[i, n]), vec(ln_b[i, n]))
        ffn_w = lambda n: (ffn_w_gate[i, n].astype(BF16), ffn_w_up[i, n].astype(BF16),
                           ffn_w_down[i, n].astype(BF16))
        x2 = _ffn(x2, *ffn_w(0), *ln(0))
        if i % 2 == 0:
            x2 = _mixer_attention(x2, bsz, s, attn_w_in[j], attn_b_f[j], attn_w_out[j], rel_bias,
                                  *ln(1))
        else:
            x2 = _sgu(x2, sgu_w_in[j].astype(BF16), vec(sgu_b_in[j]), vec(sgu_ln_g[j]),
                      vec(sgu_ln_b[j]), sgu_w_s[j], sgu_b_s[j].T, sgu_w_out[j].astype(BF16), *ln(1))
        ple = (p[i].reshape(bsz * s, -1), ple_w_gate[i].astype(BF16), ple_w_proj[i].astype(BF16))
        x2 = _ffn(x2, *ffn_w(1), *ln(2), ple=ple)
    return x2.reshape(bsz, s, d)
```

```python
import functools
import math

import jax
import jax.numpy as jnp
from jax import lax
from jax.experimental import pallas as pl
from jax.experimental.pallas import tpu as pltpu

D_MODEL = 1024
HEAD_DIM = 64
N_HEADS_A = 8
N_HEADS_B = 8
W_HEADS = N_HEADS_A * HEAD_DIM
MOBA_BLOCK = 256
MOBA_TOPK = 3
REL_BUCKETS = 32
REL_MAX_DIST = 128
SGU_CHUNK = 128
SGU_GROUPS = 8
D_FF = 2816
PLE_DIM = 256
DEPTH = 2
DN_ALPHA = (2.0 * DEPTH) ** 0.25
LN_EPS = 1e-5
NEG_INF = -1e30

F32 = jnp.float32
BF16 = jnp.bfloat16

VMEM_LIMIT_BYTES = 56 * 1024 * 1024
LANES = 128
HEADS_PER_TILE = LANES // HEAD_DIM
FF_CHUNK = 256
ROW_TILE = 512
ATTN_TILE = MOBA_BLOCK
KV_CHUNK = 512
BLOCKS_PER_CHUNK = KV_CHUNK // ATTN_TILE
N_BIAS_SLOTS = 3


def _params(*semantics):
    return pltpu.CompilerParams(dimension_semantics=semantics, vmem_limit_bytes=VMEM_LIMIT_BYTES)


def _resident(shape):
    zeros = (0,) * len(shape)
    return pl.BlockSpec(shape, lambda *_: zeros, pipeline_mode=pl.Buffered(1))


def _layer_norm(y, g, b):
    mu = jnp.mean(y, axis=-1, keepdims=True)
    d = y - mu
    var = jnp.mean(d * d, axis=-1, keepdims=True)
    return d * lax.rsqrt(var + LN_EPS) * g + b


def _dot(a, b):
    return jnp.dot(a, b, preferred_element_type=F32)


def _dot_nt(a, b):
    return lax.dot_general(a, b, (((1,), (1,)), ((), ())), preferred_element_type=F32)


def _ffn_body(x_ref, wg_ref, wu_ref, wd_ref, g_ref, b_ref, *rest, with_ple):
    if with_ple:
        p_ref, wpg_ref, wpp_ref, o_ref = rest
    else:
        (o_ref,) = rest
    x = x_ref[...]
    xb = x.astype(BF16)
    acc = jnp.zeros(x.shape, F32)
    for c in range(D_FF // FF_CHUNK):
        sl = slice(c * FF_CHUNK, (c + 1) * FF_CHUNK)
        gate = _dot(xb, wg_ref[:, sl])
        up = _dot(xb, wu_ref[:, sl])
        h = (jax.nn.silu(gate) * up).astype(BF16)
        acc = acc + _dot(h, wd_ref[sl, :])
    y = _layer_norm(DN_ALPHA * x + 0.5 * acc, g_ref[...], b_ref[...])
    if with_ple:
        gate = jax.nn.sigmoid(_dot(y.astype(BF16), wpg_ref[...]))
        y = y + gate * _dot(p_ref[...].astype(BF16), wpp_ref[...])
    o_ref[...] = y


def _ffn(x, wg, wu, wd, g, b, ple=None):
    m, d = x.shape
    tm = min(ROW_TILE, m)
    row = lambda w: pl.BlockSpec((tm, w), lambda i: (i, 0))
    in_specs = [row(d), _resident(wg.shape), _resident(wu.shape), _resident(wd.shape),
                _resident(g.shape), _resident(b.shape)]
    args = [x, wg, wu, wd, g, b]
    if ple is not None:
        p, wpg, wpp = ple
        in_specs += [row(p.shape[1]), _resident(wpg.shape), _resident(wpp.shape)]
        args += [p, wpg, wpp]
    return pl.pallas_call(
        functools.partial(_ffn_body, with_ple=ple is not None),
        grid=(m // tm,),
        in_specs=in_specs,
        out_specs=row(d),
        out_shape=jax.ShapeDtypeStruct((m, d), F32),
        compiler_params=_params("parallel"),
        name="ffn_ple" if ple is not None else "ffn",
    )(*args)


def _proj_body(x_ref, w_ref, wt_ref, qa_ref, ka_ref, qb_ref, kb_ref, vta_ref, vtb_ref, f_ref):
    xb = x_ref[0].astype(BF16)
    scale = HEAD_DIM ** -0.5

    def seg(n):
        return _dot(xb, w_ref[:, n * W_HEADS:(n + 1) * W_HEADS])

    qa_ref[0] = (seg(0) * scale).astype(BF16)
    ka_ref[0] = seg(1).astype(BF16)
    qb_ref[0] = (seg(2) * scale).astype(BF16)
    kb_ref[0] = seg(3).astype(BF16)
    rt = _dot_nt(wt_ref[...], xb)
    for n in range(vta_ref.shape[1]):
        cols = slice(n * KV_CHUNK, (n + 1) * KV_CHUNK)
        vta_ref[0, n] = rt[:W_HEADS, cols].astype(BF16)
        vtb_ref[0, n] = rt[W_HEADS:2 * W_HEADS, cols].astype(BF16)
    f_ref[0] = rt[2 * W_HEADS:, :]


def _attn_proj(x, w_rows, w_t):
    bsz, s, d = x.shape
    tm = min(ROW_TILE, s)
    n_chunk = tm // KV_CHUNK
    rows = pl.BlockSpec((1, tm, W_HEADS), lambda b, i: (b, i, 0))
    vts = pl.BlockSpec((1, n_chunk, W_HEADS, KV_CHUNK), lambda b, i: (b, i, 0, 0))
    row_t = jax.ShapeDtypeStruct((bsz, s, W_HEADS), BF16)
    vt_t = jax.ShapeDtypeStruct((bsz, s // KV_CHUNK, W_HEADS, KV_CHUNK), BF16)
    return pl.pallas_call(
        _proj_body,
        grid=(bsz, s // tm),
        in_specs=[pl.BlockSpec((1, tm, d), lambda b, i: (b, i, 0)),
                  _resident(w_rows.shape), _resident(w_t.shape)],
        out_specs=[rows, rows, rows, rows, vts, vts,
                   pl.BlockSpec((1, N_HEADS_B, tm), lambda b, i: (b, 0, i))],
        out_shape=[row_t, row_t, row_t, row_t, vt_t, vt_t,
                   jax.ShapeDtypeStruct((bsz, N_HEADS_B, s), F32)],
        compiler_params=_params("parallel", "parallel"),
        name="attn_proj",
    )(x, w_rows, w_t)


def _decay_body(f_ref, bf_ref, c_ref):
    x = f_ref[0] + bf_ref[...]
    log_f = jnp.minimum(x, 0.0) - jnp.log(1.0 + jnp.exp(-jnp.abs(x)))
    r = lax.broadcasted_iota(jnp.int32, (LANES, LANES), 0)
    c = lax.broadcasted_iota(jnp.int32, (LANES, LANES), 1)
    prefix = (r <= c).astype(F32)
    carry = jnp.zeros((x.shape[0], 1), F32)
    for n in range(x.shape[1] // LANES):
        sl = slice(n * LANES, (n + 1) * LANES)
        cs = jnp.dot(log_f[:, sl], prefix, preferred_element_type=F32,
                     precision=lax.Precision.HIGHEST) + carry
        c_ref[0, :, sl] = cs
        carry = cs[:, LANES - 1:LANES]


def _fox_decay(f_t, b_f):
    bsz, h, s = f_t.shape
    return pl.pallas_call(
        _decay_body,
        grid=(bsz,),
        in_specs=[pl.BlockSpec((1, h, s), lambda b: (b, 0, 0)), _resident((h, 1))],
        out_specs=pl.BlockSpec((1, h, s), lambda b: (b, 0, 0)),
        out_shape=jax.ShapeDtypeStruct((bsz, h, s), F32),
        compiler_params=_params("parallel"),
        name="fox_decay",
    )(f_t, b_f.reshape(h, 1))


def _bias_body(tbl_ref, o_ref):
    h = pl.program_id(0)
    r = lax.broadcasted_iota(jnp.int32, (MOBA_BLOCK, MOBA_BLOCK), 0)
    t = lax.broadcasted_iota(jnp.int32, (MOBA_BLOCK, MOBA_BLOCK), 1)
    max_exact = REL_BUCKETS // 2
    for slot in range(N_BIAS_SLOTS):
        n = jnp.maximum(t - r + slot * MOBA_BLOCK, 0)
        nf = jnp.maximum(n, max_exact).astype(F32)
        large = max_exact + (jnp.log(nf / max_exact) / math.log(REL_MAX_DIST / max_exact)
                             * (REL_BUCKETS - max_exact)).astype(jnp.int32)
        large = jnp.minimum(large, REL_BUCKETS - 1)
        bucket = jnp.where(n < max_exact, n, large)
        bias = jnp.zeros((MOBA_BLOCK, MOBA_BLOCK), F32)
        for k in range(REL_BUCKETS):
            bias = jnp.where(bucket == k, tbl_ref[k, h], bias)
        if slot == 0:
            bias = jnp.where(t >= r, bias, NEG_INF)
        o_ref[0, slot] = bias


def _moba_bias(rel_bias):
    shape = (N_HEADS_A, N_BIAS_SLOTS, MOBA_BLOCK, MOBA_BLOCK)
    return pl.pallas_call(
        _bias_body,
        grid=(N_HEADS_A,),
        in_specs=[pl.BlockSpec(memory_space=pltpu.SMEM)],
        out_specs=pl.BlockSpec((1,) + shape[1:], lambda h: (h, 0, 0, 0)),
        out_shape=jax.ShapeDtypeStruct(shape, F32),
        compiler_params=_params("parallel"),
        name="moba_bias",
    )(rel_bias)


def _head_mask(shape, head):
    lane = lax.broadcasted_iota(jnp.int32, shape, len(shape) - 1)
    return (lane // HEAD_DIM) == head


def _chunk_rows(c):
    return pl.ds(pl.multiple_of(c * KV_CHUNK, KV_CHUNK), KV_CHUNK)


def _flash_scratch():
    score_buf = pltpu.VMEM((HEADS_PER_TILE, KV_CHUNK, ATTN_TILE), F32)
    max_buf = pltpu.VMEM((HEADS_PER_TILE, 1, ATTN_TILE), F32)
    stat = pltpu.VMEM((HEADS_PER_TILE, 1, ATTN_TILE), F32)
    return [score_buf, score_buf, max_buf, max_buf, stat, stat,
            pltpu.VMEM((HEADS_PER_TILE, HEAD_DIM, ATTN_TILE), F32)]


def _flash_chunks(last, score_fn, vt_ref, o_ref, s0_ref, s1_ref, mx0_ref, mx1_ref, m_ref, l_ref,
                  acc_ref):
    heads = range(HEADS_PER_TILE)
    bufs = ((s0_ref, mx0_ref), (s1_ref, mx1_ref))
    m_ref[...] = jnp.full(m_ref.shape, NEG_INF, F32)
    l_ref[...] = jnp.zeros(l_ref.shape, F32)
    acc_ref[...] = jnp.zeros(acc_ref.shape, F32)

    def issue(c, buf):
        s_ref, mx_ref = bufs[buf]
        for h, s in enumerate(score_fn(c)):
            s_ref[h] = s
            mx_ref[h] = jnp.max(s, axis=0, keepdims=True)

    def absorb(c, buf):
        s_ref, mx_ref = bufs[buf]
        for h in heads:
            m = m_ref[h]
            m_new = jnp.maximum(m, mx_ref[h])
            p = jnp.exp(s_ref[h] - m_new)
            alpha = jnp.exp(m - m_new)
            l_ref[h] = alpha * l_ref[h] + jnp.sum(p, axis=0, keepdims=True)
            vt = vt_ref[0, c, h * HEAD_DIM:(h + 1) * HEAD_DIM, :]
            acc_ref[h] = alpha * acc_ref[h] + _dot(vt, p.astype(BF16))
            m_ref[h] = m_new

    issue(0, 0)

    def pair(n, carry):
        c = 2 * n
        issue(c + 1, 1)
        absorb(c, 0)
        issue(c + 2, 0)
        absorb(c + 1, 1)
        return carry

    lax.fori_loop(0, last // 2, pair, 0)

    @pl.when(last % 2 == 1)
    def _():
        issue(last, 1)
        absorb(last - 1, 0)
        absorb(last, 1)

    @pl.when(last % 2 == 0)
    def _():
        absorb(last, 0)

    outs = [acc_ref[h] / l_ref[h] for h in heads]
    o_ref[0] = jnp.concatenate(outs, axis=0).T.astype(o_ref.dtype)


def _fox_body(q_ref, k_ref, vt_ref, c_ref, o_ref, mask_ref, *flash_refs):
    i = pl.program_id(2)
    last = i // BLOCKS_PER_CHUNK

    @pl.when(i == 0)
    def _():
        r = lax.broadcasted_iota(jnp.int32, (KV_CHUNK, ATTN_TILE), 0)
        t = lax.broadcasted_iota(jnp.int32, (KV_CHUNK, ATTN_TILE), 1)
        for v in range(BLOCKS_PER_CHUNK + 1):
            mask_ref[v] = jnp.where(r <= t + v * ATTN_TILE, 0.0, NEG_INF)

    q = q_ref[0]
    qms = [jnp.where(_head_mask(q.shape, h), q, jnp.zeros_like(q)) for h in range(HEADS_PER_TILE)]

    def score_fn(c):
        rows = _chunk_rows(c)
        k = k_ref[0, rows, :]
        variant = jnp.where(c == last, i - last * BLOCKS_PER_CHUNK, BLOCKS_PER_CHUNK)
        mask = mask_ref[pl.ds(variant, 1)][0]
        return [_dot_nt(k, qm) - c_ref[0, 0, rows, h:h + 1] + mask for h, qm in enumerate(qms)]

    _flash_chunks(last, score_fn, vt_ref, o_ref, *flash_refs)


def _moba_body(q_ref, k_ref, vt_ref, bias_ref, o_ref, kbar_ref, sel_ref, *flash_refs):
    i = pl.program_id(2)
    n_blk = kbar_ref.shape[0]

    @pl.when(i == 0)
    def _():
        for n in range(n_blk):
            kb = k_ref[0, n * MOBA_BLOCK:(n + 1) * MOBA_BLOCK, :].astype(F32)
            kbar_ref[n:n + 1, :] = jnp.mean(kb, axis=0, keepdims=True)

    q = q_ref[0]
    qf = q.astype(F32)
    kbar = kbar_ref[...]
    blk = lax.broadcasted_iota(jnp.int32, (n_blk, ATTN_TILE), 0)
    heads = range(HEADS_PER_TILE)
    for h in heads:
        gate = lax.dot_general(jnp.where(_head_mask(kbar.shape, h), kbar, 0.0), qf,
                               (((1,), (1,)), ((), ())), preferred_element_type=F32,
                               precision=lax.Precision.HIGHEST)
        gate = jnp.where(blk < i, gate, NEG_INF)
        rank = jnp.zeros(gate.shape, jnp.int32)
        for n in range(n_blk):
            g_n = gate[n:n + 1, :]
            ahead = (g_n > gate) | ((g_n == gate) & (n < blk))
            rank = rank + ahead.astype(jnp.int32)
        keep = ((blk < i) & (rank < MOBA_TOPK)) | (blk == i)
        sel_ref[h] = jnp.where(keep, 0.0, NEG_INF)

    qms = [jnp.where(_head_mask(q.shape, h), q, jnp.zeros_like(q)) for h in heads]

    def score_fn(c):
        k = k_ref[0, _chunk_rows(c), :]
        out = []
        for h, qm in enumerate(qms):
            s = _dot_nt(k, qm)
            parts = []
            for n in range(BLOCKS_PER_CHUNK):
                j = c * BLOCKS_PER_CHUNK + n
                slot = jnp.clip(i - j, 0, N_BIAS_SLOTS - 1)
                tile = s[n * ATTN_TILE:(n + 1) * ATTN_TILE]
                parts.append(tile + bias_ref[h, pl.ds(slot, 1)][0] + sel_ref[h, pl.ds(j, 1), :])
            out.append(jnp.concatenate(parts, axis=0))
        return out

    _flash_chunks(i // BLOCKS_PER_CHUNK, score_fn, vt_ref, o_ref, *flash_refs)


def _attn_specs(bsz, s):
    n_blk = s // ATTN_TILE
    q_spec = pl.BlockSpec((1, ATTN_TILE, LANES), lambda b, p, i: (b, i, p))
    k_spec = pl.BlockSpec((1, s, LANES), lambda b, p, i: (b, 0, p))
    vt_spec = pl.BlockSpec((1, s // KV_CHUNK, LANES, KV_CHUNK), lambda b, p, i: (b, 0, p, 0))
    out_shape = jax.ShapeDtypeStruct((bsz, s, W_HEADS), BF16)
    grid = (bsz, W_HEADS // LANES, n_blk)
    return grid, q_spec, k_spec, vt_spec, out_shape


def _fox_attention(q, k, vt, c_cols):
    bsz, s, _ = q.shape
    grid, q_spec, k_spec, vt_spec, out_shape = _attn_specs(bsz, s)
    return pl.pallas_call(
        _fox_body,
        grid=grid,
        in_specs=[q_spec, k_spec, vt_spec,
                  pl.BlockSpec((1, 1, s, HEADS_PER_TILE), lambda b, p, i: (b, p, 0, 0))],
        out_specs=q_spec,
        out_shape=out_shape,
        scratch_shapes=[pltpu.VMEM((BLOCKS_PER_CHUNK + 1, KV_CHUNK, ATTN_TILE), F32)]
        + _flash_scratch(),
        compiler_params=_params("parallel", "parallel", "arbitrary"),
        name="fox_attention",
    )(q, k, vt, c_cols)


def _moba_attention(q, k, vt, bias):
    bsz, s, _ = q.shape
    grid, q_spec, k_spec, vt_spec, out_shape = _attn_specs(bsz, s)
    n_blk = s // MOBA_BLOCK
    return pl.pallas_call(
        _moba_body,
        grid=grid,
        in_specs=[q_spec, k_spec, vt_spec,
                  pl.BlockSpec((HEADS_PER_TILE, N_BIAS_SLOTS, MOBA_BLOCK, MOBA_BLOCK),
                               lambda b, p, i: (p, 0, 0, 0))],
        out_specs=q_spec,
        out_shape=out_shape,
        scratch_shapes=[pltpu.VMEM((n_blk, LANES), F32),
                        pltpu.VMEM((HEADS_PER_TILE, n_blk, ATTN_TILE), F32)] + _flash_scratch(),
        compiler_params=_params("parallel", "parallel", "arbitrary"),
        name="moba_attention",
    )(q, k, vt, bias)


def _attn_out_body(x_ref, oa_ref, ob_ref, wa_ref, wb_ref, g_ref, b_ref, o_ref):
    mixed = _dot(oa_ref[...], wa_ref[...]) + _dot(ob_ref[...], wb_ref[...])
    o_ref[...] = _layer_norm(DN_ALPHA * x_ref[...] + mixed, g_ref[...], b_ref[...])


def _attn_out(x, o_a, o_b, w_a, w_b, g, b):
    m, d = x.shape
    tm = min(ROW_TILE, m)
    row = lambda w: pl.BlockSpec((tm, w), lambda i: (i, 0))
    return pl.pallas_call(
        _attn_out_body,
        grid=(m // tm,),
        in_specs=[row(d), row(W_HEADS), row(W_HEADS), _resident(w_a.shape), _resident(w_b.shape),
                  _resident(g.shape), _resident(b.shape)],
        out_specs=row(d),
        out_shape=jax.ShapeDtypeStruct((m, d), F32),
        compiler_params=_params("parallel"),
        name="attn_out",
    )(x, o_a, o_b, w_a, w_b, g, b)


def _sgu_body(x_ref, win_ref, bin_ref, lg_ref, lb_ref, ws_ref, bs_ref, wout_ref, g_ref, b_ref,
              o_ref, y_ref):
    x = x_ref[...]
    xb = x.astype(BF16)
    w = x.shape[1]
    u = jax.nn.gelu(_dot(xb, win_ref[:, :w]) + bin_ref[:, :w])
    v = jax.nn.gelu(_dot(xb, win_ref[:, w:]) + bin_ref[:, w:])
    vb = _layer_norm(v, lg_ref[...], lb_ref[...]).astype(BF16)
    t = lax.broadcasted_iota(jnp.int32, (SGU_CHUNK, SGU_CHUNK), 0)
    s = lax.broadcasted_iota(jnp.int32, (SGU_CHUNK, SGU_CHUNK), 1)
    gw = w // SGU_GROUPS
    for g in range(SGU_GROUPS):
        w_g = jnp.where(t >= s, ws_ref[g], 0.0).astype(BF16)
        cols = slice(g * gw, (g + 1) * gw)
        for c in range(x.shape[0] // SGU_CHUNK):
            rows = slice(c * SGU_CHUNK, (c + 1) * SGU_CHUNK)
            mixed = _dot(w_g, vb[rows, cols]) + bs_ref[:, g:g + 1]
            y_ref[rows, cols] = (u[rows, cols] * mixed).astype(BF16)
    out = _dot(y_ref[...], wout_ref[...])
    o_ref[...] = _layer_norm(DN_ALPHA * x + out, g_ref[...], b_ref[...])


def _sgu(x, w_in, b_in, ln_g, ln_b, w_s, b_s_t, w_out, g, b):
    m, d = x.shape
    tm = min(ROW_TILE, m)
    row = pl.BlockSpec((tm, d), lambda i: (i, 0))
    weights = [w_in, b_in, ln_g, ln_b, w_s, b_s_t, w_out, g, b]
    return pl.pallas_call(
        _sgu_body,
        grid=(m // tm,),
        in_specs=[row] + [_resident(a.shape) for a in weights],
        out_specs=row,
        out_shape=jax.ShapeDtypeStruct((m, d), F32),
        scratch_shapes=[pltpu.VMEM((tm, d), BF16)],
        compiler_params=_params("parallel"),
        name="sgu",
    )(x, *weights)


def _mixer_attention(x2, bsz, s, w_in, b_f, w_out, rel_bias, g, b):
    d = x2.shape[1]
    seg = lambda n: w_in[:, n * W_HEADS:(n + 1) * W_HEADS]
    w_rows = jnp.concatenate([seg(0), seg(1), seg(3), seg(4)], axis=1).astype(BF16)
    w_t = jnp.concatenate([seg(2), seg(5), w_in[:, 6 * W_HEADS:]], axis=1).T.astype(BF16)
    qa, ka, qb, kb, vta, vtb, f_t = _attn_proj(x2.reshape(bsz, s, d), w_rows, w_t)
    c = _fox_decay(f_t, b_f)
    c_cols = c.reshape(bsz, N_HEADS_B // HEADS_PER_TILE, HEADS_PER_TILE, s).transpose(0, 1, 3, 2)
    o_a = _moba_attention(qa, ka, vta, _moba_bias(rel_bias))
    o_b = _fox_attention(qb, kb, vtb, c_cols)
    w_out = w_out.astype(BF16)
    return _attn_out(x2, o_a.reshape(bsz * s, W_HEADS), o_b.reshape(bsz * s, W_HEADS),
                     w_out[:W_HEADS], w_out[W_HEADS:], g, b)


def kernel(x, p, ln_g, ln_b, ffn_w_gate, ffn_w_up, ffn_w_down, attn_w_in, attn_b_f, attn_w_out,
           rel_bias, sgu_w_in, sgu_b_in, sgu_ln_g, sgu_ln_b, sgu_w_s, sgu_b_s, sgu_w_out,
           ple_w_proj, ple_w_gate):
    bsz, s, d = x.shape
    x2 = x.reshape(bsz * s, d)
    vec = lambda a: a.reshape(1, -1)
    for i in range(DEPTH):
        j = i // 2
        ln = lambda n: (vec(ln_g[i, n]), vec(ln_b[i, n]))
        ffn_w = lambda n: (ffn_w_gate[i, n].astype(BF16), ffn_w_up[i, n].astype(BF16),
                           ffn_w_down[i, n].astype(BF16))
        x2 = _ffn(x2, *ffn_w(0), *ln(0))
        if i % 2 == 0:
            x2 = _mixer_attention(x2, bsz, s, attn_w_in[j], attn_b_f[j], attn_w_out[j], rel_bias,
                                  *ln(1))
        else:
            x2 = _sgu(x2, sgu_w_in[j].astype(BF16), vec(sgu_b_in[j]), vec(sgu_ln_g[j]),
                      vec(sgu_ln_b[j]), sgu_w_s[j], sgu_b_s[j].T, sgu_w_out[j].astype(BF16), *ln(1))
        ple = (p[i].reshape(bsz * s, -1), ple_w_gate[i].astype(BF16), ple_w_proj[i].astype(BF16))
        x2 = _ffn(x2, *ffn_w(1), *ln(2), ple=ple)
    return x2.reshape(bsz, s, d)
```

```python
import functools
import math

import jax
import jax.numpy as jnp
from jax import lax
from jax.experimental import pallas as pl
from jax.experimental.pallas import tpu as pltpu

D_MODEL = 1024
HEAD_DIM = 64
N_HEADS_A = 8
N_HEADS_B = 8
W_HEADS = N_HEADS_A * HEAD_DIM
MOBA_BLOCK = 256
MOBA_TOPK = 3
REL_BUCKETS = 32
REL_MAX_DIST = 128
SGU_CHUNK = 128
SGU_GROUPS = 8
D_FF = 2816
PLE_DIM = 256
DEPTH = 2
DN_ALPHA = (2.0 * DEPTH) ** 0.25
LN_EPS = 1e-5
NEG_INF = -1e30
LOG2_E = math.log2(math.e)

F32 = jnp.float32
BF16 = jnp.bfloat16

VMEM_LIMIT_BYTES = 56 * 1024 * 1024
LANES = 128
HEADS_PER_TILE = LANES // HEAD_DIM
HEADS_PER_STEP = 4
STEP_LANES = HEADS_PER_STEP * HEAD_DIM
FF_CHUNK = 256
ROW_TILE = 512
ATTN_TILE = MOBA_BLOCK
KV_CHUNK = 512
BLOCKS_PER_CHUNK = KV_CHUNK // ATTN_TILE
N_BIAS_SLOTS = 3
DECAY_PARTS = 3
ONES_ROWS = 16


def _params(*semantics):
    return pltpu.CompilerParams(dimension_semantics=semantics, vmem_limit_bytes=VMEM_LIMIT_BYTES)


def _resident(shape):
    zeros = (0,) * len(shape)
    return pl.BlockSpec(shape, lambda *_: zeros, pipeline_mode=pl.Buffered(1))


def _layer_norm(y, g, b):
    mu = jnp.mean(y, axis=-1, keepdims=True)
    d = y - mu
    var = jnp.mean(d * d, axis=-1, keepdims=True)
    return d * lax.rsqrt(var + LN_EPS) * g + b


def _dot(a, b):
    return jnp.dot(a, b, preferred_element_type=F32)


def _dot_nt(a, b):
    return lax.dot_general(a, b, (((1,), (1,)), ((), ())), preferred_element_type=F32)


def _ffn_body(x_ref, wg_ref, wu_ref, wd_ref, g_ref, b_ref, *rest, with_ple):
    if with_ple:
        p_ref, wpg_ref, wpp_ref, o_ref = rest
    else:
        (o_ref,) = rest
    x = x_ref[...]
    xb = x.astype(BF16)
    acc = jnp.zeros(x.shape, F32)
    for c in range(D_FF // FF_CHUNK):
        sl = slice(c * FF_CHUNK, (c + 1) * FF_CHUNK)
        gate = _dot(xb, wg_ref[:, sl])
        up = _dot(xb, wu_ref[:, sl])
        h = (jax.nn.silu(gate) * up).astype(BF16)
        acc = acc + _dot(h, wd_ref[sl, :])
    y = _layer_norm(DN_ALPHA * x + 0.5 * acc, g_ref[...], b_ref[...])
    if with_ple:
        gate = jax.nn.sigmoid(_dot(y.astype(BF16), wpg_ref[...]))
        y = y + gate * _dot(p_ref[...].astype(BF16), wpp_ref[...])
    o_ref[...] = y


def _ffn(x, wg, wu, wd, g, b, ple=None):
    m, d = x.shape
    tm = min(ROW_TILE, m)
    row = lambda w: pl.BlockSpec((tm, w), lambda i: (i, 0))
    in_specs = [row(d), _resident(wg.shape), _resident(wu.shape), _resident(wd.shape),
                _resident(g.shape), _resident(b.shape)]
    args = [x, wg, wu, wd, g, b]
    if ple is not None:
        p, wpg, wpp = ple
        in_specs += [row(p.shape[1]), _resident(wpg.shape), _resident(wpp.shape)]
        args += [p, wpg, wpp]
    return pl.pallas_call(
        functools.partial(_ffn_body, with_ple=ple is not None),
        grid=(m // tm,),
        in_specs=in_specs,
        out_specs=row(d),
        out_shape=jax.ShapeDtypeStruct((m, d), F32),
        compiler_params=_params("parallel"),
        name="ffn_ple" if ple is not None else "ffn",
    )(*args)


def _proj_body(x_ref, w_ref, wt_ref, qa_ref, ka_ref, qb_ref, kb_ref, vta_ref, vtb_ref, f_ref):
    xb = x_ref[0].astype(BF16)
    scale = HEAD_DIM ** -0.5 * LOG2_E

    def seg(n):
        return _dot(xb, w_ref[:, n * W_HEADS:(n + 1) * W_HEADS])

    qa_ref[0] = (seg(0) * scale).astype(BF16)
    ka_ref[0] = seg(1).astype(BF16)
    qb_ref[0] = (seg(2) * scale).astype(BF16)
    kb_ref[0] = seg(3).astype(BF16)
    rt = _dot_nt(wt_ref[...], xb)
    for n in range(vta_ref.shape[1]):
        cols = slice(n * KV_CHUNK, (n + 1) * KV_CHUNK)
        vta_ref[0, n] = rt[:W_HEADS, cols].astype(BF16)
        vtb_ref[0, n] = rt[W_HEADS:2 * W_HEADS, cols].astype(BF16)
    f_ref[0] = rt[2 * W_HEADS:, :]


def _attn_proj(x, w_rows, w_t):
    bsz, s, d = x.shape
    tm = min(ROW_TILE, s)
    n_chunk = tm // KV_CHUNK
    rows = pl.BlockSpec((1, tm, W_HEADS), lambda b, i: (b, i, 0))
    vts = pl.BlockSpec((1, n_chunk, W_HEADS, KV_CHUNK), lambda b, i: (b, i, 0, 0))
    row_t = jax.ShapeDtypeStruct((bsz, s, W_HEADS), BF16)
    vt_t = jax.ShapeDtypeStruct((bsz, s // KV_CHUNK, W_HEADS, KV_CHUNK), BF16)
    return pl.pallas_call(
        _proj_body,
        grid=(bsz, s // tm),
        in_specs=[pl.BlockSpec((1, tm, d), lambda b, i: (b, i, 0)),
                  _resident(w_rows.shape), _resident(w_t.shape)],
        out_specs=[rows, rows, rows, rows, vts, vts,
                   pl.BlockSpec((1, N_HEADS_B, tm), lambda b, i: (b, 0, i))],
        out_shape=[row_t, row_t, row_t, row_t, vt_t, vt_t,
                   jax.ShapeDtypeStruct((bsz, N_HEADS_B, s), F32)],
        compiler_params=_params("parallel", "parallel"),
        name="attn_proj",
    )(x, w_rows, w_t)


def _decay_body(f_ref, bf_ref, c_ref):
    x = f_ref[0] + bf_ref[...]
    log_f = jnp.minimum(x, 0.0) - jnp.log(1.0 + jnp.exp(-jnp.abs(x)))
    r = lax.broadcasted_iota(jnp.int32, (LANES, LANES), 0)
    c = lax.broadcasted_iota(jnp.int32, (LANES, LANES), 1)
    prefix = (r <= c).astype(F32)
    carry = jnp.zeros((x.shape[0], 1), F32)
    for n in range(x.shape[1] // LANES):
        sl = slice(n * LANES, (n + 1) * LANES)
        cs = jnp.dot(log_f[:, sl], prefix, preferred_element_type=F32,
                     precision=lax.Precision.HIGHEST) + carry
        carry = cs[:, LANES - 1:LANES]
        rest = cs * LOG2_E
        for part in range(DECAY_PARTS):
            term = rest.astype(BF16).astype(F32)
            c_ref[0, part, :, sl] = term
            rest = rest - term


def _fox_decay(f_t, b_f):
    bsz, h, s = f_t.shape
    return pl.pallas_call(
        _decay_body,
        grid=(bsz,),
        in_specs=[pl.BlockSpec((1, h, s), lambda b: (b, 0, 0)), _resident((h, 1))],
        out_specs=pl.BlockSpec((1, DECAY_PARTS, h, s), lambda b: (b, 0, 0, 0)),
        out_shape=jax.ShapeDtypeStruct((bsz, DECAY_PARTS, h, s), F32),
        compiler_params=_params("parallel"),
        name="fox_decay",
    )(f_t, b_f.reshape(h, 1))


def _bias_body(tbl_ref, o_ref):
    h = pl.program_id(0)
    r = lax.broadcasted_iota(jnp.int32, (MOBA_BLOCK, MOBA_BLOCK), 0)
    t = lax.broadcasted_iota(jnp.int32, (MOBA_BLOCK, MOBA_BLOCK), 1)
    max_exact = REL_BUCKETS // 2
    for slot in range(N_BIAS_SLOTS):
        n = jnp.maximum(t - r + slot * MOBA_BLOCK, 0)
        nf = jnp.maximum(n, max_exact).astype(F32)
        large = max_exact + (jnp.log(nf / max_exact) / math.log(REL_MAX_DIST / max_exact)
                             * (REL_BUCKETS - max_exact)).astype(jnp.int32)
        large = jnp.minimum(large, REL_BUCKETS - 1)
        bucket = jnp.where(n < max_exact, n, large)
        bias = jnp.zeros((MOBA_BLOCK, MOBA_BLOCK), F32)
        for k in range(REL_BUCKETS):
            bias = jnp.where(bucket == k, tbl_ref[k, h], bias)
        bias = bias * LOG2_E
        if slot == 0:
            bias = jnp.where(t >= r, bias, NEG_INF)
        o_ref[0, slot] = bias


def _moba_bias(rel_bias):
    shape = (N_HEADS_A, N_BIAS_SLOTS, MOBA_BLOCK, MOBA_BLOCK)
    return pl.pallas_call(
        _bias_body,
        grid=(N_HEADS_A,),
        in_specs=[pl.BlockSpec(memory_space=pltpu.SMEM)],
        out_specs=pl.BlockSpec((1,) + shape[1:], lambda h: (h, 0, 0, 0)),
        out_shape=jax.ShapeDtypeStruct(shape, F32),
        compiler_params=_params("parallel"),
        name="moba_bias",
    )(rel_bias)


def _lane_tile(x, head):
    tile = head // HEADS_PER_TILE
    return x[:, tile * LANES:(tile + 1) * LANES]


def _head_only(x, head):
    xt = _lane_tile(x, head)
    lane = lax.broadcasted_iota(jnp.int32, xt.shape, 1)
    return jnp.where(lane // HEAD_DIM == head % HEADS_PER_TILE, xt, jnp.zeros_like(xt))


def _chunk_rows(c):
    return pl.ds(pl.multiple_of(c * KV_CHUNK, KV_CHUNK), KV_CHUNK)


def _flash_scratch():
    score_buf = pltpu.VMEM((HEADS_PER_STEP, KV_CHUNK, ATTN_TILE), F32)
    max_buf = pltpu.VMEM((HEADS_PER_STEP, 1, ATTN_TILE), F32)
    return [score_buf, score_buf, max_buf, max_buf,
            pltpu.VMEM((HEADS_PER_STEP, 1, ATTN_TILE), F32),
            pltpu.VMEM((HEADS_PER_STEP, HEAD_DIM + ONES_ROWS, ATTN_TILE), F32)]


def _flash_chunks(last, score_fn, vt_ref, o_ref, s0_ref, s1_ref, mx0_ref, mx1_ref, m_ref, acc_ref):
    heads = range(HEADS_PER_STEP)
    bufs = ((s0_ref, mx0_ref), (s1_ref, mx1_ref))
    m_ref[...] = jnp.full(m_ref.shape, NEG_INF, F32)
    acc_ref[...] = jnp.zeros(acc_ref.shape, F32)
    ones = jnp.ones((ONES_ROWS, KV_CHUNK), BF16)

    def issue(c, kind, buf):
        s_ref, mx_ref = bufs[buf]
        for h, s in enumerate(score_fn(c, kind)):
            s_ref[h] = s
            mx_ref[h] = jnp.max(s, axis=0, keepdims=True)

    def absorb(c, buf):
        s_ref, mx_ref = bufs[buf]
        for h in heads:
            m = m_ref[h]
            m_new = jnp.maximum(m, mx_ref[h])
            p = jnp.exp2(s_ref[h] - m_new).astype(BF16)
            vt = jnp.concatenate([vt_ref[0, c, h * HEAD_DIM:(h + 1) * HEAD_DIM, :], ones], axis=0)
            acc_ref[h] = jnp.exp2(m - m_new) * acc_ref[h] + _dot(vt, p)
            m_ref[h] = m_new

    issue(last, "diag", 0)

    @pl.when(last == 0)
    def _():
        absorb(last, 0)

    @pl.when(last > 0)
    def _():
        issue(last - 1, "near", 1)
        absorb(last, 0)
        n_far = last - 1

        def pair(n, carry):
            c = last - 2 - 2 * n
            issue(c, "far", 0)
            absorb(c + 1, 1)
            issue(c - 1, "far", 1)
            absorb(c, 0)
            return carry

        lax.fori_loop(0, n_far // 2, pair, 0)

        @pl.when(n_far % 2 == 1)
        def _():
            issue(0, "far", 0)
            absorb(1, 1)
            absorb(0, 0)

        @pl.when(n_far % 2 == 0)
        def _():
            absorb(0, 1)

    outs = [acc_ref[h, :HEAD_DIM] / acc_ref[h, HEAD_DIM:HEAD_DIM + 1] for h in heads]
    o_ref[0] = jnp.concatenate(outs, axis=0).T.astype(o_ref.dtype)


def _fox_body(q_ref, k_ref, cp_ref, vt_ref, o_ref, mask_ref, *flash_refs):
    i = pl.program_id(2)
    last = i // BLOCKS_PER_CHUNK

    @pl.when(i == 0)
    def _():
        r = lax.broadcasted_iota(jnp.int32, (KV_CHUNK, ATTN_TILE), 0)
        t = lax.broadcasted_iota(jnp.int32, (KV_CHUNK, ATTN_TILE), 1)
        for v in range(BLOCKS_PER_CHUNK):
            mask_ref[v] = jnp.where(r <= t + v * ATTN_TILE, 0.0, NEG_INF)

    q = q_ref[0]
    lane = lax.broadcasted_iota(jnp.int32, (ATTN_TILE, LANES), 1)
    qxs = []
    for h in range(HEADS_PER_STEP):
        on_terms = (lane >= h * DECAY_PARTS) & (lane < (h + 1) * DECAY_PARTS)
        qxs.append(jnp.concatenate([_head_only(q, h), jnp.where(on_terms, -1.0, 0.0).astype(BF16)],
                                   axis=1))

    def score_fn(c, kind):
        rows = _chunk_rows(c)
        k = k_ref[0, rows, :]
        terms = cp_ref[0, 0, rows, :]
        scores = [_dot_nt(jnp.concatenate([_lane_tile(k, h), terms], axis=1), qx)
                  for h, qx in enumerate(qxs)]
        if kind == "diag":
            mask = mask_ref[pl.ds(i - last * BLOCKS_PER_CHUNK, 1)][0]
            scores = [s + mask for s in scores]
        return scores

    _flash_chunks(last, score_fn, vt_ref, o_ref, *flash_refs)


def _moba_body(q_ref, k_ref, vt_ref, bias_ref, o_ref, kbar_ref, sel_ref, far_ref, *flash_refs):
    i = pl.program_id(2)
    n_blk = kbar_ref.shape[0]

    @pl.when(i == 0)
    def _():
        for n in range(n_blk):
            kb = k_ref[0, n * MOBA_BLOCK:(n + 1) * MOBA_BLOCK, :].astype(F32)
            kbar_ref[n:n + 1, :] = jnp.mean(kb, axis=0, keepdims=True)

    q = q_ref[0]
    qf = q.astype(F32)
    kbar = kbar_ref[...]
    blk = lax.broadcasted_iota(jnp.int32, (n_blk, ATTN_TILE), 0)
    heads = range(HEADS_PER_STEP)
    for h in heads:
        gate = lax.dot_general(_head_only(kbar, h), _lane_tile(qf, h),
                               (((1,), (1,)), ((), ())), preferred_element_type=F32,
                               precision=lax.Precision.HIGHEST)
        gate = jnp.where(blk < i, gate, NEG_INF)
        keep = blk == i
        for _ in range(MOBA_TOPK):
            top = jnp.max(gate, axis=0, keepdims=True)
            first = jnp.min(jnp.where(gate == top, blk, n_blk), axis=0, keepdims=True)
            hit = blk == first
            keep = keep | (hit & (blk < i))
            gate = jnp.where(hit, -jnp.inf, gate)
        sel = jnp.where(keep, 0.0, NEG_INF)
        sel_ref[h] = sel
        far_ref[h] = sel + bias_ref[h, N_BIAS_SLOTS - 1, 0:1, 0:1]

    qms = [_head_only(q, h) for h in heads]

    def score_fn(c, kind):
        k = k_ref[0, _chunk_rows(c), :]
        out = []
        for h, qm in enumerate(qms):
            s = _dot_nt(_lane_tile(k, h), qm)
            parts = []
            for n in range(BLOCKS_PER_CHUNK):
                j = c * BLOCKS_PER_CHUNK + n
                tile = s[n * ATTN_TILE:(n + 1) * ATTN_TILE]
                if kind == "far":
                    parts.append(tile + far_ref[h, pl.ds(j, 1), :])
                else:
                    slot = jnp.clip(i - j, 0, N_BIAS_SLOTS - 1)
                    parts.append(tile + bias_ref[h, pl.ds(slot, 1)][0] + sel_ref[h, pl.ds(j, 1), :])
            out.append(jnp.concatenate(parts, axis=0))
        return out

    _flash_chunks(i // BLOCKS_PER_CHUNK, score_fn, vt_ref, o_ref, *flash_refs)


def _attn_specs(bsz, s):
    n_blk = s // ATTN_TILE
    q_spec = pl.BlockSpec((1, ATTN_TILE, STEP_LANES), lambda b, p, i: (b, i, p))
    k_spec = pl.BlockSpec((1, s, STEP_LANES), lambda b, p, i: (b, 0, p))
    vt_spec = pl.BlockSpec((1, s // KV_CHUNK, STEP_LANES, KV_CHUNK), lambda b, p, i: (b, 0, p, 0))
    out_shape = jax.ShapeDtypeStruct((bsz, s, W_HEADS), BF16)
    grid = (bsz, W_HEADS // STEP_LANES, n_blk)
    return grid, q_spec, k_spec, vt_spec, out_shape


def _fox_attention(q, k, decay_terms, vt):
    bsz, s, _ = q.shape
    grid, q_spec, k_spec, vt_spec, out_shape = _attn_specs(bsz, s)
    return pl.pallas_call(
        _fox_body,
        grid=grid,
        in_specs=[q_spec, k_spec, pl.BlockSpec((1, 1, s, LANES), lambda b, p, i: (b, p, 0, 0)),
                  vt_spec],
        out_specs=q_spec,
        out_shape=out_shape,
        scratch_shapes=[pltpu.VMEM((BLOCKS_PER_CHUNK, KV_CHUNK, ATTN_TILE), F32)]
        + _flash_scratch(),
        compiler_params=_params("parallel", "parallel", "arbitrary"),
        name="fox_attention",
    )(q, k, decay_terms, vt)


def _moba_attention(q, k, vt, bias):
    bsz, s, _ = q.shape
    grid, q_spec, k_spec, vt_spec, out_shape = _attn_specs(bsz, s)
    n_blk = s // MOBA_BLOCK
    return pl.pallas_call(
        _moba_body,
        grid=grid,
        in_specs=[q_spec, k_spec, vt_spec,
                  pl.BlockSpec((HEADS_PER_STEP, N_BIAS_SLOTS, MOBA_BLOCK, MOBA_BLOCK),
                               lambda b, p, i: (p, 0, 0, 0))],
        out_specs=q_spec,
        out_shape=out_shape,
        scratch_shapes=[pltpu.VMEM((n_blk, STEP_LANES), F32),
                        pltpu.VMEM((HEADS_PER_STEP, n_blk, ATTN_TILE), F32),
                        pltpu.VMEM((HEADS_PER_STEP, n_blk, ATTN_TILE), F32)] + _flash_scratch(),
        compiler_params=_params("parallel", "parallel", "arbitrary"),
        name="moba_attention",
    )(q, k, vt, bias)


def _attn_out_body(x_ref, oa_ref, ob_ref, wa_ref, wb_ref, g_ref, b_ref, o_ref):
    mixed = _dot(oa_ref[...], wa_ref[...]) + _dot(ob_ref[...], wb_ref[...])
    o_ref[...] = _layer_norm(DN_ALPHA * x_ref[...] + mixed, g_ref[...], b_ref[...])


def _attn_out(x, o_a, o_b, w_a, w_b, g, b):
    m, d = x.shape
    tm = min(ROW_TILE, m)
    row = lambda w: pl.BlockSpec((tm, w), lambda i: (i, 0))
    return pl.pallas_call(
        _attn_out_body,
        grid=(m // tm,),
        in_specs=[row(d), row(W_HEADS), row(W_HEADS), _resident(w_a.shape), _resident(w_b.shape),
                  _resident(g.shape), _resident(b.shape)],
        out_specs=row(d),
        out_shape=jax.ShapeDtypeStruct((m, d), F32),
        compiler_params=_params("parallel"),
        name="attn_out",
    )(x, o_a, o_b, w_a, w_b, g, b)


def _sgu_body(x_ref, win_ref, bin_ref, lg_ref, lb_ref, ws_ref, bs_ref, wout_ref, g_ref, b_ref,
              o_ref, y_ref):
    x = x_ref[...]
    xb = x.astype(BF16)
    w = x.shape[1]
    u = jax.nn.gelu(_dot(xb, win_ref[:, :w]) + bin_ref[:, :w])
    v = jax.nn.gelu(_dot(xb, win_ref[:, w:]) + bin_ref[:, w:])
    vb = _layer_norm(v, lg_ref[...], lb_ref[...]).astype(BF16)
    t = lax.broadcasted_iota(jnp.int32, (SGU_CHUNK, SGU_CHUNK), 0)
    s = lax.broadcasted_iota(jnp.int32, (SGU_CHUNK, SGU_CHUNK), 1)
    gw = w // SGU_GROUPS
    for g in range(SGU_GROUPS):
        w_g = jnp.where(t >= s, ws_ref[g], 0.0).astype(BF16)
        cols = slice(g * gw, (g + 1) * gw)
        for c in range(x.shape[0] // SGU_CHUNK):
            rows = slice(c * SGU_CHUNK, (c + 1) * SGU_CHUNK)
            mixed = _dot(w_g, vb[rows, cols]) + bs_ref[:, g:g + 1]
            y_ref[rows, cols] = (u[rows, cols] * mixed).astype(BF16)
    out = _dot(y_ref[...], wout_ref[...])
    o_ref[...] = _layer_norm(DN_ALPHA * x + out, g_ref[...], b_ref[...])


def _sgu(x, w_in, b_in, ln_g, ln_b, w_s, b_s_t, w_out, g, b):
    m, d = x.shape
    tm = min(ROW_TILE, m)
    row = pl.BlockSpec((tm, d), lambda i: (i, 0))
    weights = [w_in, b_in, ln_g, ln_b, w_s, b_s_t, w_out, g, b]
    return pl.pallas_call(
        _sgu_body,
        grid=(m // tm,),
        in_specs=[row] + [_resident(a.shape) for a in weights],
        out_specs=row,
        out_shape=jax.ShapeDtypeStruct((m, d), F32),
        scratch_shapes=[pltpu.VMEM((tm, d), BF16)],
        compiler_params=_params("parallel"),
        name="sgu",
    )(x, *weights)


def _mixer_attention(x2, bsz, s, w_in, b_f, w_out, rel_bias, g, b):
    d = x2.shape[1]
    seg = lambda n: w_in[:, n * W_HEADS:(n + 1) * W_HEADS]
    w_rows = jnp.concatenate([seg(0), seg(1), seg(3), seg(4)], axis=1).astype(BF16)
    w_t = jnp.concatenate([seg(2), seg(5), w_in[:, 6 * W_HEADS:]], axis=1).T.astype(BF16)
    qa, ka, qb, kb, vta, vtb, f_t = _attn_proj(x2.reshape(bsz, s, d), w_rows, w_t)
    n_groups = N_HEADS_B // HEADS_PER_STEP
    terms = _fox_decay(f_t, b_f).reshape(bsz, DECAY_PARTS, n_groups, HEADS_PER_STEP, s)
    terms = terms.transpose(0, 2, 4, 3, 1).reshape(bsz, n_groups, s, HEADS_PER_STEP * DECAY_PARTS)
    terms = jnp.pad(terms.astype(BF16), ((0, 0),) * 3 + ((0, LANES - terms.shape[-1]),))
    o_a = _moba_attention(qa, ka, vta, _moba_bias(rel_bias))
    o_b = _fox_attention(qb, kb, terms, vtb)
    w_out = w_out.astype(BF16)
    return _attn_out(x2, o_a.reshape(bsz * s, W_HEADS), o_b.reshape(bsz * s, W_HEADS),
                     w_out[:W_HEADS], w_out[W_HEADS:], g, b)


def kernel(x, p, ln_g, ln_b, ffn_w_gate, ffn_w_up, ffn_w_down, attn_w_in, attn_b_f, attn_w_out,
           rel_bias, sgu_w_in, sgu_b_in, sgu_ln_g, sgu_ln_b, sgu_w_s, sgu_b_s, sgu_w_out,
           ple_w_proj, ple_w_gate):
    bsz, s, d = x.shape
    x2 = x.reshape(bsz * s, d)
    vec = lambda a: a.reshape(1, -1)
    for i in range(DEPTH):
        j = i // 2
        ln = lambda n: (vec(ln_g[i, n]), vec(ln_b[i, n]))
        ffn_w = lambda n: (ffn_w_gate[i, n].astype(BF16), ffn_w_up[i, n].astype(BF16),
                           ffn_w_down[i, n].astype(BF16))
        x2 = _ffn(x2, *ffn_w(0), *ln(0))
        if i % 2 == 0:
            x2 = _mixer_attention(x2, bsz, s, attn_w_in[j], attn_b_f[j], attn_w_out[j], rel_bias,
                                  *ln(1))
        else:
            x2 = _sgu(x2, sgu_w_in[j].astype(BF16), vec(sgu_b_in[j]), vec(sgu_ln_g[j]),
                      vec(sgu_ln_b[j]), sgu_w_s[j], sgu_b_s[j].T, sgu_w_out[j].astype(BF16), *ln(1))
        ple = (p[i].reshape(bsz * s, -1), ple_w_gate[i].astype(BF16), ple_w_proj[i].astype(BF16))
        x2 = _ffn(x2, *ffn_w(1), *ln(2), ple=ple)
    return x2.reshape(bsz, s, d)
```

```python
import functools
import math

import jax
import jax.numpy as jnp
from jax import lax
from jax.experimental import pallas as pl
from jax.experimental.pallas import tpu as pltpu

D_MODEL = 1024
HEAD_DIM = 64
N_HEADS_A = 8
N_HEADS_B = 8
W_HEADS = N_HEADS_A * HEAD_DIM
MOBA_BLOCK = 256
MOBA_TOPK = 3
REL_BUCKETS = 32
REL_MAX_DIST = 128
SGU_CHUNK = 128
SGU_GROUPS = 8
D_FF = 2816
PLE_DIM = 256
DEPTH = 2
DN_ALPHA = (2.0 * DEPTH) ** 0.25
LN_EPS = 1e-5
NEG_INF = -1e30
LOG2_E = math.log2(math.e)

F32 = jnp.float32
BF16 = jnp.bfloat16

VMEM_LIMIT_BYTES = 56 * 1024 * 1024
LANES = 128
HEADS_PER_TILE = LANES // HEAD_DIM
HEADS_PER_STEP = 8
STEP_LANES = HEADS_PER_STEP * HEAD_DIM
FF_CHUNK = 256
ROW_TILE = 512
ATTN_TILE = MOBA_BLOCK
KV_CHUNK = 512
BLOCKS_PER_CHUNK = KV_CHUNK // ATTN_TILE
N_BIAS_SLOTS = 3
DECAY_PARTS = 3
ONES_ROWS = 16


def _params(*semantics):
    return pltpu.CompilerParams(dimension_semantics=semantics, vmem_limit_bytes=VMEM_LIMIT_BYTES)


def _resident(shape):
    zeros = (0,) * len(shape)
    return pl.BlockSpec(shape, lambda *_: zeros, pipeline_mode=pl.Buffered(1))


def _layer_norm(y, g, b):
    mu = jnp.mean(y, axis=-1, keepdims=True)
    d = y - mu
    var = jnp.mean(d * d, axis=-1, keepdims=True)
    return d * lax.rsqrt(var + LN_EPS) * g + b


def _dot(a, b):
    return jnp.dot(a, b, preferred_element_type=F32)


def _dot_nt(a, b):
    return lax.dot_general(a, b, (((1,), (1,)), ((), ())), preferred_element_type=F32)


def _ffn_body(x_ref, wg_ref, wu_ref, wd_ref, g_ref, b_ref, *rest, with_ple):
    if with_ple:
        p_ref, wpg_ref, wpp_ref, o_ref = rest
    else:
        (o_ref,) = rest
    x = x_ref[...]
    xb = x.astype(BF16)
    acc = jnp.zeros(x.shape, F32)
    for c in range(D_FF // FF_CHUNK):
        sl = slice(c * FF_CHUNK, (c + 1) * FF_CHUNK)
        gate = _dot(xb, wg_ref[:, sl])
        up = _dot(xb, wu_ref[:, sl])
        h = (jax.nn.silu(gate) * up).astype(BF16)
        acc = acc + _dot(h, wd_ref[sl, :])
    y = _layer_norm(DN_ALPHA * x + 0.5 * acc, g_ref[...], b_ref[...])
    if with_ple:
        gate = jax.nn.sigmoid(_dot(y.astype(BF16), wpg_ref[...]))
        y = y + gate * _dot(p_ref[...].astype(BF16), wpp_ref[...])
    o_ref[...] = y


def _ffn(x, wg, wu, wd, g, b, ple=None):
    m, d = x.shape
    tm = min(ROW_TILE, m)
    row = lambda w: pl.BlockSpec((tm, w), lambda i: (i, 0))
    in_specs = [row(d), _resident(wg.shape), _resident(wu.shape), _resident(wd.shape),
                _resident(g.shape), _resident(b.shape)]
    args = [x, wg, wu, wd, g, b]
    if ple is not None:
        p, wpg, wpp = ple
        in_specs += [row(p.shape[1]), _resident(wpg.shape), _resident(wpp.shape)]
        args += [p, wpg, wpp]
    return pl.pallas_call(
        functools.partial(_ffn_body, with_ple=ple is not None),
        grid=(m // tm,),
        in_specs=in_specs,
        out_specs=row(d),
        out_shape=jax.ShapeDtypeStruct((m, d), F32),
        compiler_params=_params("parallel"),
        name="ffn_ple" if ple is not None else "ffn",
    )(*args)


def _proj_body(x_ref, w_ref, wt_ref, qa_ref, ka_ref, qb_ref, kb_ref, vta_ref, vtb_ref, f_ref):
    xb = x_ref[0].astype(BF16)
    scale = HEAD_DIM ** -0.5 * LOG2_E

    def seg(n):
        return _dot(xb, w_ref[:, n * W_HEADS:(n + 1) * W_HEADS])

    qa_ref[0] = (seg(0) * scale).astype(BF16)
    ka_ref[0] = seg(1).astype(BF16)
    qb_ref[0] = (seg(2) * scale).astype(BF16)
    kb_ref[0] = seg(3).astype(BF16)
    rt = _dot_nt(wt_ref[...], xb)
    for n in range(vta_ref.shape[1]):
        cols = slice(n * KV_CHUNK, (n + 1) * KV_CHUNK)
        vta_ref[0, n] = rt[:W_HEADS, cols].astype(BF16)
        vtb_ref[0, n] = rt[W_HEADS:2 * W_HEADS, cols].astype(BF16)
    f_ref[0] = rt[2 * W_HEADS:, :]


def _attn_proj(x, w_rows, w_t):
    bsz, s, d = x.shape
    tm = min(ROW_TILE, s)
    n_chunk = tm // KV_CHUNK
    rows = pl.BlockSpec((1, tm, W_HEADS), lambda b, i: (b, i, 0))
    vts = pl.BlockSpec((1, n_chunk, W_HEADS, KV_CHUNK), lambda b, i: (b, i, 0, 0))
    row_t = jax.ShapeDtypeStruct((bsz, s, W_HEADS), BF16)
    vt_t = jax.ShapeDtypeStruct((bsz, s // KV_CHUNK, W_HEADS, KV_CHUNK), BF16)
    return pl.pallas_call(
        _proj_body,
        grid=(bsz, s // tm),
        in_specs=[pl.BlockSpec((1, tm, d), lambda b, i: (b, i, 0)),
                  _resident(w_rows.shape), _resident(w_t.shape)],
        out_specs=[rows, rows, rows, rows, vts, vts,
                   pl.BlockSpec((1, N_HEADS_B, tm), lambda b, i: (b, 0, i))],
        out_shape=[row_t, row_t, row_t, row_t, vt_t, vt_t,
                   jax.ShapeDtypeStruct((bsz, N_HEADS_B, s), F32)],
        compiler_params=_params("parallel", "parallel"),
        name="attn_proj",
    )(x, w_rows, w_t)


def _decay_body(f_ref, bf_ref, c_ref):
    x = f_ref[0] + bf_ref[...]
    log_f = jnp.minimum(x, 0.0) - jnp.log(1.0 + jnp.exp(-jnp.abs(x)))
    r = lax.broadcasted_iota(jnp.int32, (LANES, LANES), 0)
    c = lax.broadcasted_iota(jnp.int32, (LANES, LANES), 1)
    prefix = (r <= c).astype(F32)
    carry = jnp.zeros((x.shape[0], 1), F32)
    for n in range(x.shape[1] // LANES):
        sl = slice(n * LANES, (n + 1) * LANES)
        cs = jnp.dot(log_f[:, sl], prefix, preferred_element_type=F32,
                     precision=lax.Precision.HIGHEST) + carry
        carry = cs[:, LANES - 1:LANES]
        rest = cs * LOG2_E
        for part in range(DECAY_PARTS):
            term = rest.astype(BF16).astype(F32)
            c_ref[0, part, :, sl] = term
            rest = rest - term


def _fox_decay(f_t, b_f):
    bsz, h, s = f_t.shape
    return pl.pallas_call(
        _decay_body,
        grid=(bsz,),
        in_specs=[pl.BlockSpec((1, h, s), lambda b: (b, 0, 0)), _resident((h, 1))],
        out_specs=pl.BlockSpec((1, DECAY_PARTS, h, s), lambda b: (b, 0, 0, 0)),
        out_shape=jax.ShapeDtypeStruct((bsz, DECAY_PARTS, h, s), F32),
        compiler_params=_params("parallel"),
        name="fox_decay",
    )(f_t, b_f.reshape(h, 1))


def _bias_body(tbl_ref, o_ref):
    h = pl.program_id(0)
    r = lax.broadcasted_iota(jnp.int32, (MOBA_BLOCK, MOBA_BLOCK), 0)
    t = lax.broadcasted_iota(jnp.int32, (MOBA_BLOCK, MOBA_BLOCK), 1)
    max_exact = REL_BUCKETS // 2
    for slot in range(N_BIAS_SLOTS):
        n = jnp.maximum(t - r + slot * MOBA_BLOCK, 0)
        nf = jnp.maximum(n, max_exact).astype(F32)
        large = max_exact + (jnp.log(nf / max_exact) / math.log(REL_MAX_DIST / max_exact)
                             * (REL_BUCKETS - max_exact)).astype(jnp.int32)
        large = jnp.minimum(large, REL_BUCKETS - 1)
        bucket = jnp.where(n < max_exact, n, large)
        bias = jnp.zeros((MOBA_BLOCK, MOBA_BLOCK), F32)
        for k in range(REL_BUCKETS):
            bias = jnp.where(bucket == k, tbl_ref[k, h], bias)
        bias = bias * LOG2_E
        if slot == 0:
            bias = jnp.where(t >= r, bias, NEG_INF)
        o_ref[0, slot] = bias


def _moba_bias(rel_bias):
    shape = (N_HEADS_A, N_BIAS_SLOTS, MOBA_BLOCK, MOBA_BLOCK)
    return pl.pallas_call(
        _bias_body,
        grid=(N_HEADS_A,),
        in_specs=[pl.BlockSpec(memory_space=pltpu.SMEM)],
        out_specs=pl.BlockSpec((1,) + shape[1:], lambda h: (h, 0, 0, 0)),
        out_shape=jax.ShapeDtypeStruct(shape, F32),
        compiler_params=_params("parallel"),
        name="moba_bias",
    )(rel_bias)


def _lane_tile(x, head):
    tile = head // HEADS_PER_TILE
    return x[:, tile * LANES:(tile + 1) * LANES]


def _head_only(x, head):
    xt = _lane_tile(x, head)
    lane = lax.broadcasted_iota(jnp.int32, xt.shape, 1)
    return jnp.where(lane // HEAD_DIM == head % HEADS_PER_TILE, xt, jnp.zeros_like(xt))


def _chunk_rows(c):
    return pl.ds(pl.multiple_of(c * KV_CHUNK, KV_CHUNK), KV_CHUNK)


def _flash_scratch():
    score_buf = pltpu.VMEM((HEADS_PER_STEP, KV_CHUNK, ATTN_TILE), F32)
    max_buf = pltpu.VMEM((HEADS_PER_STEP, 1, ATTN_TILE), F32)
    return [score_buf] * 3 + [max_buf] * 3 + [
        pltpu.VMEM((HEADS_PER_STEP, 1, ATTN_TILE), F32),
        pltpu.VMEM((HEADS_PER_STEP, HEAD_DIM + ONES_ROWS, ATTN_TILE), F32)]


def _next_tile_spec(n_tiles):
    return pl.BlockSpec((1, ATTN_TILE, STEP_LANES),
                        lambda b, p, i: (b, jnp.minimum(i + 1, n_tiles - 1), p))


def _flash_tile(i, n_tiles, prepare, scorer, q_ref, qnext_ref, vt_ref, o_ref,
                s0_ref, s1_ref, sd_ref, mx0_ref, mx1_ref, mxd_ref, m_ref, acc_ref):
    heads = range(HEADS_PER_STEP)
    diag_buf = 2
    bufs = ((s0_ref, mx0_ref), (s1_ref, mx1_ref), (sd_ref, mxd_ref))
    last = i // BLOCKS_PER_CHUNK
    slot = i % 2
    ones = jnp.ones((ONES_ROWS, KV_CHUNK), BF16)

    def issue(score_fn, c, kind, buf):
        s_ref, mx_ref = bufs[buf]
        for h, s in enumerate(score_fn(c, kind)):
            s_ref[h] = s
            mx_ref[h] = jnp.max(s, axis=0, keepdims=True)

    def absorb(c, buf):
        s_ref, mx_ref = bufs[buf]
        for h in heads:
            m = m_ref[h]
            m_new = jnp.maximum(m, mx_ref[h])
            p = jnp.exp2(s_ref[h] - m_new).astype(BF16)
            vt = jnp.concatenate([vt_ref[0, c, h * HEAD_DIM:(h + 1) * HEAD_DIM, :], ones], axis=0)
            acc_ref[h] = jnp.exp2(m - m_new) * acc_ref[h] + _dot(vt, p)
            m_ref[h] = m_new

    def start_tile(t, q_tile_ref, t_slot):
        prepare(t, q_tile_ref, t_slot)
        issue(scorer(t, q_tile_ref, t_slot), t // BLOCKS_PER_CHUNK, "diag", diag_buf)

    def start_next():
        start_tile(jnp.minimum(i + 1, n_tiles - 1), qnext_ref, 1 - slot)

    @pl.when(i == 0)
    def _():
        start_tile(i, q_ref, slot)

    m_ref[...] = jnp.full(m_ref.shape, NEG_INF, F32)
    acc_ref[...] = jnp.zeros(acc_ref.shape, F32)
    score_fn = scorer(i, q_ref, slot)

    @pl.when(last == 0)
    def _():
        absorb(last, diag_buf)
        start_next()

    @pl.when(last > 0)
    def _():
        issue(score_fn, last - 1, "near", 1)
        absorb(last, diag_buf)
        n_far = last - 1

        def pair(n, carry):
            c = last - 2 - 2 * n
            issue(score_fn, c, "far", 0)
            absorb(c + 1, 1)
            issue(score_fn, c - 1, "far", 1)
            absorb(c, 0)
            return carry

        lax.fori_loop(0, n_far // 2, pair, 0)

        @pl.when(n_far % 2 == 1)
        def _():
            issue(score_fn, 0, "far", 0)
            absorb(1, 1)
            start_next()
            absorb(0, 0)

        @pl.when(n_far % 2 == 0)
        def _():
            start_next()
            absorb(0, 1)

    outs = [acc_ref[h, :HEAD_DIM] / acc_ref[h, HEAD_DIM:HEAD_DIM + 1] for h in heads]
    o_ref[0] = jnp.concatenate(outs, axis=0).T.astype(o_ref.dtype)


def _fox_body(q_ref, qnext_ref, k_ref, cp_ref, vt_ref, o_ref, mask_ref, *flash_refs):
    i = pl.program_id(2)

    @pl.when(i == 0)
    def _():
        r = lax.broadcasted_iota(jnp.int32, (KV_CHUNK, ATTN_TILE), 0)
        t = lax.broadcasted_iota(jnp.int32, (KV_CHUNK, ATTN_TILE), 1)
        for v in range(BLOCKS_PER_CHUNK):
            mask_ref[v] = jnp.where(r <= t + v * ATTN_TILE, 0.0, NEG_INF)

    def scorer(t, q_tile_ref, slot):
        q = q_tile_ref[0]
        lane = lax.broadcasted_iota(jnp.int32, (ATTN_TILE, LANES), 1)
        qxs = []
        for h in range(HEADS_PER_STEP):
            on_terms = (lane >= h * DECAY_PARTS) & (lane < (h + 1) * DECAY_PARTS)
            minus_one = jnp.where(on_terms, -1.0, 0.0).astype(BF16)
            qxs.append(jnp.concatenate([_head_only(q, h), minus_one], axis=1))

        def score_fn(c, kind):
            rows = _chunk_rows(c)
            k = k_ref[0, rows, :]
            terms = cp_ref[0, 0, rows, :]
            scores = [_dot_nt(jnp.concatenate([_lane_tile(k, h), terms], axis=1), qx)
                      for h, qx in enumerate(qxs)]
            if kind == "diag":
                mask = mask_ref[pl.ds(t % BLOCKS_PER_CHUNK, 1)][0]
                scores = [s + mask for s in scores]
            return scores

        return score_fn

    _flash_tile(i, pl.num_programs(2), lambda t, q_tile_ref, slot: None, scorer, q_ref, qnext_ref,
                vt_ref, o_ref, *flash_refs)


def _moba_body(q_ref, qnext_ref, k_ref, vt_ref, bias_ref, o_ref, kbar_ref, sel_ref, far_ref,
               *flash_refs):
    i = pl.program_id(2)
    n_blk = kbar_ref.shape[0]
    heads = range(HEADS_PER_STEP)

    @pl.when(i == 0)
    def _():
        for n in range(n_blk):
            kb = k_ref[0, n * MOBA_BLOCK:(n + 1) * MOBA_BLOCK, :].astype(F32)
            kbar_ref[n:n + 1, :] = jnp.mean(kb, axis=0, keepdims=True)

    def prepare(t, q_tile_ref, slot):
        qf = q_tile_ref[0].astype(F32)
        kbar = kbar_ref[...]
        blk = lax.broadcasted_iota(jnp.int32, (n_blk, ATTN_TILE), 0)
        for h in heads:
            gate = lax.dot_general(_head_only(kbar, h), _lane_tile(qf, h),
                                   (((1,), (1,)), ((), ())), preferred_element_type=F32,
                                   precision=lax.Precision.HIGHEST)
            gate = jnp.where(blk < t, gate, NEG_INF)
            keep = blk == t
            for _ in range(MOBA_TOPK):
                top = jnp.max(gate, axis=0, keepdims=True)
                first = jnp.min(jnp.where(gate == top, blk, n_blk), axis=0, keepdims=True)
                hit = blk == first
                keep = keep | (hit & (blk < t))
                gate = jnp.where(hit, -jnp.inf, gate)
            sel = jnp.where(keep, 0.0, NEG_INF)
            sel_ref[slot, h] = sel
            far_ref[slot, h] = sel + bias_ref[h, N_BIAS_SLOTS - 1, 0:1, 0:1]

    def scorer(t, q_tile_ref, slot):
        q = q_tile_ref[0]
        qms = [_head_only(q, h) for h in heads]

        def score_fn(c, kind):
            k = k_ref[0, _chunk_rows(c), :]
            out = []
            for h, qm in enumerate(qms):
                s = _dot_nt(_lane_tile(k, h), qm)
                parts = []
                for n in range(BLOCKS_PER_CHUNK):
                    j = c * BLOCKS_PER_CHUNK + n
                    tile = s[n * ATTN_TILE:(n + 1) * ATTN_TILE]
                    if kind == "far":
                        parts.append(tile + far_ref[slot, h, pl.ds(j, 1), :])
                    else:
                        bias = bias_ref[h, pl.ds(jnp.clip(t - j, 0, N_BIAS_SLOTS - 1), 1)][0]
                        parts.append(tile + bias + sel_ref[slot, h, pl.ds(j, 1), :])
                out.append(jnp.concatenate(parts, axis=0))
            return out

        return score_fn

    _flash_tile(i, pl.num_programs(2), prepare, scorer, q_ref, qnext_ref, vt_ref, o_ref,
                *flash_refs)


def _attn_specs(bsz, s):
    n_blk = s // ATTN_TILE
    q_spec = pl.BlockSpec((1, ATTN_TILE, STEP_LANES), lambda b, p, i: (b, i, p))
    once = pl.Buffered(1)
    k_spec = pl.BlockSpec((1, s, STEP_LANES), lambda b, p, i: (b, 0, p), pipeline_mode=once)
    vt_spec = pl.BlockSpec((1, s // KV_CHUNK, STEP_LANES, KV_CHUNK), lambda b, p, i: (b, 0, p, 0),
                           pipeline_mode=once)
    out_shape = jax.ShapeDtypeStruct((bsz, s, W_HEADS), BF16)
    grid = (bsz, W_HEADS // STEP_LANES, n_blk)
    return grid, q_spec, k_spec, vt_spec, out_shape


def _fox_attention(q, k, decay_terms, vt):
    bsz, s, _ = q.shape
    grid, q_spec, k_spec, vt_spec, out_shape = _attn_specs(bsz, s)
    return pl.pallas_call(
        _fox_body,
        grid=grid,
        in_specs=[q_spec, _next_tile_spec(grid[2]), k_spec,
                  pl.BlockSpec((1, 1, s, LANES), lambda b, p, i: (b, p, 0, 0),
                               pipeline_mode=pl.Buffered(1)),
                  vt_spec],
        out_specs=q_spec,
        out_shape=out_shape,
        scratch_shapes=[pltpu.VMEM((BLOCKS_PER_CHUNK, KV_CHUNK, ATTN_TILE), F32)]
        + _flash_scratch(),
        compiler_params=_params("parallel", "parallel", "arbitrary"),
        name="fox_attention",
    )(q, q, k, decay_terms, vt)


def _moba_attention(q, k, vt, bias):
    bsz, s, _ = q.shape
    grid, q_spec, k_spec, vt_spec, out_shape = _attn_specs(bsz, s)
    n_blk = s // MOBA_BLOCK
    return pl.pallas_call(
        _moba_body,
        grid=grid,
        in_specs=[q_spec, _next_tile_spec(grid[2]), k_spec, vt_spec,
                  pl.BlockSpec((HEADS_PER_STEP, N_BIAS_SLOTS, MOBA_BLOCK, MOBA_BLOCK),
                               lambda b, p, i: (p, 0, 0, 0), pipeline_mode=pl.Buffered(1))],
        out_specs=q_spec,
        out_shape=out_shape,
        scratch_shapes=[pltpu.VMEM((n_blk, STEP_LANES), F32),
                        pltpu.VMEM((2, HEADS_PER_STEP, n_blk, ATTN_TILE), F32),
                        pltpu.VMEM((2, HEADS_PER_STEP, n_blk, ATTN_TILE), F32)] + _flash_scratch(),
        compiler_params=_params("parallel", "parallel", "arbitrary"),
        name="moba_attention",
    )(q, q, k, vt, bias)


def _attn_out_body(x_ref, oa_ref, ob_ref, wa_ref, wb_ref, g_ref, b_ref, o_ref):
    mixed = _dot(oa_ref[...], wa_ref[...]) + _dot(ob_ref[...], wb_ref[...])
    o_ref[...] = _layer_norm(DN_ALPHA * x_ref[...] + mixed, g_ref[...], b_ref[...])


def _attn_out(x, o_a, o_b, w_a, w_b, g, b):
    m, d = x.shape
    tm = min(ROW_TILE, m)
    row = lambda w: pl.BlockSpec((tm, w), lambda i: (i, 0))
    return pl.pallas_call(
        _attn_out_body,
        grid=(m // tm,),
        in_specs=[row(d), row(W_HEADS), row(W_HEADS), _resident(w_a.shape), _resident(w_b.shape),
                  _resident(g.shape), _resident(b.shape)],
        out_specs=row(d),
        out_shape=jax.ShapeDtypeStruct((m, d), F32),
        compiler_params=_params("parallel"),
        name="attn_out",
    )(x, o_a, o_b, w_a, w_b, g, b)


def _sgu_body(x_ref, win_ref, bin_ref, lg_ref, lb_ref, ws_ref, bs_ref, wout_ref, g_ref, b_ref,
              o_ref, y_ref):
    x = x_ref[...]
    xb = x.astype(BF16)
    w = x.shape[1]
    u = jax.nn.gelu(_dot(xb, win_ref[:, :w]) + bin_ref[:, :w])
    v = jax.nn.gelu(_dot(xb, win_ref[:, w:]) + bin_ref[:, w:])
    vb = _layer_norm(v, lg_ref[...], lb_ref[...]).astype(BF16)
    t = lax.broadcasted_iota(jnp.int32, (SGU_CHUNK, SGU_CHUNK), 0)
    s = lax.broadcasted_iota(jnp.int32, (SGU_CHUNK, SGU_CHUNK), 1)
    gw = w // SGU_GROUPS
    for g in range(SGU_GROUPS):
        w_g = jnp.where(t >= s, ws_ref[g], 0.0).astype(BF16)
        cols = slice(g * gw, (g + 1) * gw)
        for c in range(x.shape[0] // SGU_CHUNK):
            rows = slice(c * SGU_CHUNK, (c + 1) * SGU_CHUNK)
            mixed = _dot(w_g, vb[rows, cols]) + bs_ref[:, g:g + 1]
            y_ref[rows, cols] = (u[rows, cols] * mixed).astype(BF16)
    out = _dot(y_ref[...], wout_ref[...])
    o_ref[...] = _layer_norm(DN_ALPHA * x + out, g_ref[...], b_ref[...])


def _sgu(x, w_in, b_in, ln_g, ln_b, w_s, b_s_t, w_out, g, b):
    m, d = x.shape
    tm = min(ROW_TILE, m)
    row = pl.BlockSpec((tm, d), lambda i: (i, 0))
    weights = [w_in, b_in, ln_g, ln_b, w_s, b_s_t, w_out, g, b]
    return pl.pallas_call(
        _sgu_body,
        grid=(m // tm,),
        in_specs=[row] + [_resident(a.shape) for a in weights],
        out_specs=row,
        out_shape=jax.ShapeDtypeStruct((m, d), F32),
        scratch_shapes=[pltpu.VMEM((tm, d), BF16)],
        compiler_params=_params("parallel"),
        name="sgu",
    )(x, *weights)


def _mixer_attention(x2, bsz, s, w_in, b_f, w_out, rel_bias, g, b):
    d = x2.shape[1]
    seg = lambda n: w_in[:, n * W_HEADS:(n + 1) * W_HEADS]
    w_rows = jnp.concatenate([seg(0), seg(1), seg(3), seg(4)], axis=1).astype(BF16)
    w_t = jnp.concatenate([seg(2), seg(5), w_in[:, 6 * W_HEADS:]], axis=1).T.astype(BF16)
    qa, ka, qb, kb, vta, vtb, f_t = _attn_proj(x2.reshape(bsz, s, d), w_rows, w_t)
    n_groups = N_HEADS_B // HEADS_PER_STEP
    terms = _fox_decay(f_t, b_f).reshape(bsz, DECAY_PARTS, n_groups, HEADS_PER_STEP, s)
    terms = terms.transpose(0, 2, 4, 3, 1).reshape(bsz, n_groups, s, HEADS_PER_STEP * DECAY_PARTS)
    terms = jnp.pad(terms.astype(BF16), ((0, 0),) * 3 + ((0, LANES - terms.shape[-1]),))
    o_a = _moba_attention(qa, ka, vta, _moba_bias(rel_bias))
    o_b = _fox_attention(qb, kb, terms, vtb)
    w_out = w_out.astype(BF16)
    return _attn_out(x2, o_a.reshape(bsz * s, W_HEADS), o_b.reshape(bsz * s, W_HEADS),
                     w_out[:W_HEADS], w_out[W_HEADS:], g, b)


def kernel(x, p, ln_g, ln_b, ffn_w_gate, ffn_w_up, ffn_w_down, attn_w_in, attn_b_f, attn_w_out,
           rel_bias, sgu_w_in, sgu_b_in, sgu_ln_g, sgu_ln_b, sgu_w_s, sgu_b_s, sgu_w_out,
           ple_w_proj, ple_w_gate):
    bsz, s, d = x.shape
    x2 = x.reshape(bsz * s, d)
    vec = lambda a: a.reshape(1, -1)
    for i in range(DEPTH):
        j = i // 2
        ln = lambda n: (vec(ln_g[i, n]), vec(ln_b[i, n]))
        ffn_w = lambda n: (ffn_w_gate[i, n].astype(BF16), ffn_w_up[i, n].astype(BF16),
                           ffn_w_down[i, n].astype(BF16))
        x2 = _ffn(x2, *ffn_w(0), *ln(0))
        if i % 2 == 0:
            x2 = _mixer_attention(x2, bsz, s, attn_w_in[j], attn_b_f[j], attn_w_out[j], rel_bias,
                                  *ln(1))
        else:
            x2 = _sgu(x2, sgu_w_in[j].astype(BF16), vec(sgu_b_in[j]), vec(sgu_ln_g[j]),
                      vec(sgu_ln_b[j]), sgu_w_s[j], sgu_b_s[j].T, sgu_w_out[j].astype(BF16), *ln(1))
        ple = (p[i].reshape(bsz * s, -1), ple_w_gate[i].astype(BF16), ple_w_proj[i].astype(BF16))
        x2 = _ffn(x2, *ffn_w(1), *ln(2), ple=ple)
    return x2.reshape(bsz, s, d)
```

```python
import functools
import math

import jax
import jax.numpy as jnp
from jax import lax
from jax.experimental import pallas as pl
from jax.experimental.pallas import tpu as pltpu

D_MODEL = 1024
HEAD_DIM = 64
N_HEADS_A = 8
N_HEADS_B = 8
W_HEADS = N_HEADS_A * HEAD_DIM
MOBA_BLOCK = 256
MOBA_TOPK = 3
REL_BUCKETS = 32
REL_MAX_DIST = 128
SGU_CHUNK = 128
SGU_GROUPS = 8
D_FF = 2816
PLE_DIM = 256
DEPTH = 2
DN_ALPHA = (2.0 * DEPTH) ** 0.25
LN_EPS = 1e-5
NEG_INF = -1e30
LOG2_E = math.log2(math.e)

F32 = jnp.float32
BF16 = jnp.bfloat16

VMEM_LIMIT_BYTES = 56 * 1024 * 1024
LANES = 128
HEADS_PER_TILE = LANES // HEAD_DIM
HEADS_PER_STEP = 8
STEP_LANES = HEADS_PER_STEP * HEAD_DIM
FF_CHUNK = 256
ROW_TILE = 512
ATTN_TILE = MOBA_BLOCK
KV_CHUNK = 512
BLOCKS_PER_CHUNK = KV_CHUNK // ATTN_TILE
N_BIAS_SLOTS = 3
DECAY_PARTS = 3
ONES_ROWS = 16


def _params(*semantics):
    return pltpu.CompilerParams(dimension_semantics=semantics, vmem_limit_bytes=VMEM_LIMIT_BYTES)


def _resident(shape):
    zeros = (0,) * len(shape)
    return pl.BlockSpec(shape, lambda *_: zeros, pipeline_mode=pl.Buffered(1))


def _pick(array, *lead):
    tail = array.shape[len(lead):]
    index = tuple(lead) + (0,) * len(tail)
    spec = pl.BlockSpec((None,) * len(lead) + tail, lambda *_: index, pipeline_mode=pl.Buffered(1))
    return array, spec


def _layer_norm(y, g, b):
    mu = jnp.mean(y, axis=-1, keepdims=True)
    d = y - mu
    var = jnp.mean(d * d, axis=-1, keepdims=True)
    return d * lax.rsqrt(var + LN_EPS) * g + b


def _dot(a, b):
    return jnp.dot(a, b, preferred_element_type=F32)


def _dot_nt(a, b):
    return lax.dot_general(a, b, (((1,), (1,)), ((), ())), preferred_element_type=F32)


def _ffn_body(x_ref, *refs, with_mix, with_ple):
    refs = list(refs)
    o_ref = refs.pop()
    x = x_ref[...]
    if with_mix:
        oa_ref, ob_ref, wa_ref, wb_ref, mg_ref, mb_ref = refs[:6]
        refs = refs[6:]
        mixed = _dot(oa_ref[...], wa_ref[...]) + _dot(ob_ref[...], wb_ref[...])
        x = _layer_norm(DN_ALPHA * x + mixed, mg_ref[...], mb_ref[...])
    wg_ref, wu_ref, wd_ref, g_ref, b_ref = refs[:5]
    if with_ple:
        p_ref, wpg_ref, wpp_ref = refs[5:]
    xb = x.astype(BF16)
    acc = jnp.zeros(x.shape, F32)
    for c in range(D_FF // FF_CHUNK):
        sl = slice(c * FF_CHUNK, (c + 1) * FF_CHUNK)
        gate = _dot(xb, wg_ref[:, sl])
        up = _dot(xb, wu_ref[:, sl])
        h = (jax.nn.silu(gate) * up).astype(BF16)
        acc = acc + _dot(h, wd_ref[sl, :])
    y = _layer_norm(DN_ALPHA * x + 0.5 * acc, g_ref[...], b_ref[...])
    if with_ple:
        gate = jax.nn.sigmoid(_dot(y.astype(BF16), wpg_ref[...]))
        y = y + gate * _dot(p_ref[...].astype(BF16), wpp_ref[...])
    o_ref[...] = y


def _ffn(x, weights, mix=None, ple=None):
    m, d = x.shape
    tm = min(ROW_TILE, m)
    row = lambda w: pl.BlockSpec((tm, w), lambda i: (i, 0))
    args, in_specs = [x], [row(d)]

    def add(picks):
        for array, spec in picks:
            args.append(array)
            in_specs.append(spec)

    if mix is not None:
        o_a, o_b = mix[:2]
        add([(o_a, row(o_a.shape[1])), (o_b, row(o_b.shape[1]))])
        add(mix[2:])
    add(weights)
    if ple is not None:
        (p, layer), wpg, wpp = ple
        add([(p, pl.BlockSpec((None, tm, p.shape[2]), lambda i: (layer, i, 0))), wpg, wpp])
    return pl.pallas_call(
        functools.partial(_ffn_body, with_mix=mix is not None, with_ple=ple is not None),
        grid=(m // tm,),
        in_specs=in_specs,
        out_specs=row(d),
        out_shape=jax.ShapeDtypeStruct((m, d), F32),
        compiler_params=_params("parallel"),
        name="ffn" + ("_mix" if mix is not None else "") + ("_ple" if ple is not None else ""),
    )(*args)


def _proj_body(x_ref, w_ref, wt_ref, qa_ref, ka_ref, qb_ref, kb_ref, vta_ref, vtb_ref, f_ref):
    xb = x_ref[0].astype(BF16)
    scale = HEAD_DIM ** -0.5 * LOG2_E

    def seg(n):
        return _dot(xb, w_ref[:, n * W_HEADS:(n + 1) * W_HEADS])

    qa_ref[0] = (seg(0) * scale).astype(BF16)
    ka_ref[0] = seg(1).astype(BF16)
    qb_ref[0] = (seg(2) * scale).astype(BF16)
    kb_ref[0] = seg(3).astype(BF16)
    rt = _dot_nt(wt_ref[...], xb)
    for n in range(vta_ref.shape[1]):
        cols = slice(n * KV_CHUNK, (n + 1) * KV_CHUNK)
        vta_ref[0, n] = rt[:W_HEADS, cols].astype(BF16)
        vtb_ref[0, n] = rt[W_HEADS:2 * W_HEADS, cols].astype(BF16)
    f_ref[0] = rt[2 * W_HEADS:, :]


def _attn_proj(x, w_rows, w_t):
    bsz, s, d = x.shape
    tm = min(ROW_TILE, s)
    n_chunk = tm // KV_CHUNK
    rows = pl.BlockSpec((1, tm, W_HEADS), lambda b, i: (b, i, 0))
    vts = pl.BlockSpec((1, n_chunk, W_HEADS, KV_CHUNK), lambda b, i: (b, i, 0, 0))
    row_t = jax.ShapeDtypeStruct((bsz, s, W_HEADS), BF16)
    vt_t = jax.ShapeDtypeStruct((bsz, s // KV_CHUNK, W_HEADS, KV_CHUNK), BF16)
    return pl.pallas_call(
        _proj_body,
        grid=(bsz, s // tm),
        in_specs=[pl.BlockSpec((1, tm, d), lambda b, i: (b, i, 0)),
                  _resident(w_rows.shape), _resident(w_t.shape)],
        out_specs=[rows, rows, rows, rows, vts, vts,
                   pl.BlockSpec((1, N_HEADS_B, tm), lambda b, i: (b, 0, i))],
        out_shape=[row_t, row_t, row_t, row_t, vt_t, vt_t,
                   jax.ShapeDtypeStruct((bsz, N_HEADS_B, s), F32)],
        compiler_params=_params("parallel", "parallel"),
        name="attn_proj",
    )(x, w_rows, w_t)


def _decay_body(f_ref, bf_ref, c_ref):
    x = f_ref[0] + bf_ref[...]
    log_f = jnp.minimum(x, 0.0) - jnp.log(1.0 + jnp.exp(-jnp.abs(x)))
    r = lax.broadcasted_iota(jnp.int32, (LANES, LANES), 0)
    c = lax.broadcasted_iota(jnp.int32, (LANES, LANES), 1)
    prefix = (r <= c).astype(F32)
    carry = jnp.zeros((x.shape[0], 1), F32)
    for n in range(x.shape[1] // LANES):
        sl = slice(n * LANES, (n + 1) * LANES)
        cs = jnp.dot(log_f[:, sl], prefix, preferred_element_type=F32,
                     precision=lax.Precision.HIGHEST) + carry
        carry = cs[:, LANES - 1:LANES]
        rest = cs * LOG2_E
        for part in range(DECAY_PARTS):
            term = rest.astype(BF16).astype(F32)
            c_ref[0, part, :, sl] = term
            rest = rest - term


def _fox_decay(f_t, b_f):
    bsz, h, s = f_t.shape
    return pl.pallas_call(
        _decay_body,
        grid=(bsz,),
        in_specs=[pl.BlockSpec((1, h, s), lambda b: (b, 0, 0)), _resident((h, 1))],
        out_specs=pl.BlockSpec((1, DECAY_PARTS, h, s), lambda b: (b, 0, 0, 0)),
        out_shape=jax.ShapeDtypeStruct((bsz, DECAY_PARTS, h, s), F32),
        compiler_params=_params("parallel"),
        name="fox_decay",
    )(f_t, b_f.reshape(h, 1))


def _bias_body(tbl_ref, o_ref):
    h = pl.program_id(0)
    r = lax.broadcasted_iota(jnp.int32, (MOBA_BLOCK, MOBA_BLOCK), 0)
    t = lax.broadcasted_iota(jnp.int32, (MOBA_BLOCK, MOBA_BLOCK), 1)
    max_exact = REL_BUCKETS // 2
    for slot in range(N_BIAS_SLOTS):
        n = jnp.maximum(t - r + slot * MOBA_BLOCK, 0)
        nf = jnp.maximum(n, max_exact).astype(F32)
        large = max_exact + (jnp.log(nf / max_exact) / math.log(REL_MAX_DIST / max_exact)
                             * (REL_BUCKETS - max_exact)).astype(jnp.int32)
        large = jnp.minimum(large, REL_BUCKETS - 1)
        bucket = jnp.where(n < max_exact, n, large)
        bias = jnp.zeros((MOBA_BLOCK, MOBA_BLOCK), F32)
        for k in range(REL_BUCKETS):
            bias = jnp.where(bucket == k, tbl_ref[k, h], bias)
        bias = bias * LOG2_E
        if slot == 0:
            bias = jnp.where(t >= r, bias, NEG_INF)
        o_ref[0, slot] = bias


def _moba_bias(rel_bias):
    shape = (N_HEADS_A, N_BIAS_SLOTS, MOBA_BLOCK, MOBA_BLOCK)
    return pl.pallas_call(
        _bias_body,
        grid=(N_HEADS_A,),
        in_specs=[pl.BlockSpec(memory_space=pltpu.SMEM)],
        out_specs=pl.BlockSpec((1,) + shape[1:], lambda h: (h, 0, 0, 0)),
        out_shape=jax.ShapeDtypeStruct(shape, F32),
        compiler_params=_params("parallel"),
        name="moba_bias",
    )(rel_bias)


def _lane_tile(x, head):
    tile = head // HEADS_PER_TILE
    return x[:, tile * LANES:(tile + 1) * LANES]


def _head_only(x, head):
    xt = _lane_tile(x, head)
    lane = lax.broadcasted_iota(jnp.int32, xt.shape, 1)
    return jnp.where(lane // HEAD_DIM == head % HEADS_PER_TILE, xt, jnp.zeros_like(xt))


def _chunk_rows(c):
    return pl.ds(pl.multiple_of(c * KV_CHUNK, KV_CHUNK), KV_CHUNK)


def _flash_scratch():
    score_buf = pltpu.VMEM((HEADS_PER_STEP, KV_CHUNK, ATTN_TILE), F32)
    max_buf = pltpu.VMEM((HEADS_PER_STEP, 1, ATTN_TILE), F32)
    return [score_buf] * 3 + [max_buf] * 3 + [
        pltpu.VMEM((HEADS_PER_STEP, 1, ATTN_TILE), F32),
        pltpu.VMEM((HEADS_PER_STEP, HEAD_DIM + ONES_ROWS, ATTN_TILE), F32)]


def _next_tile_spec(n_tiles):
    return pl.BlockSpec((1, ATTN_TILE, STEP_LANES),
                        lambda b, p, i: (b, jnp.minimum(i + 1, n_tiles - 1), p))


def _flash_tile(i, n_tiles, prepare, scorer, q_ref, qnext_ref, vt_ref, o_ref,
                s0_ref, s1_ref, sd_ref, mx0_ref, mx1_ref, mxd_ref, m_ref, acc_ref):
    heads = range(HEADS_PER_STEP)
    diag_buf = 2
    bufs = ((s0_ref, mx0_ref), (s1_ref, mx1_ref), (sd_ref, mxd_ref))
    last = i // BLOCKS_PER_CHUNK
    slot = i % 2
    ones = jnp.ones((ONES_ROWS, KV_CHUNK), BF16)

    def issue(score_fn, c, kind, buf):
        s_ref, mx_ref = bufs[buf]
        for h, s in enumerate(score_fn(c, kind)):
            s_ref[h] = s
            mx_ref[h] = jnp.max(s, axis=0, keepdims=True)

    def absorb(c, buf):
        s_ref, mx_ref = bufs[buf]
        for h in heads:
            m = m_ref[h]
            m_new = jnp.maximum(m, mx_ref[h])
            p = jnp.exp2(s_ref[h] - m_new).astype(BF16)
            vt = jnp.concatenate([vt_ref[0, c, h * HEAD_DIM:(h + 1) * HEAD_DIM, :], ones], axis=0)
            acc_ref[h] = jnp.exp2(m - m_new) * acc_ref[h] + _dot(vt, p)
            m_ref[h] = m_new

    def start_tile(t, q_tile_ref, t_slot):
        prepare(t, q_tile_ref, t_slot)
        issue(scorer(t, q_tile_ref, t_slot), t // BLOCKS_PER_CHUNK, "diag", diag_buf)

    def start_next():
        start_tile(jnp.minimum(i + 1, n_tiles - 1), qnext_ref, 1 - slot)

    @pl.when(i == 0)
    def _():
        start_tile(i, q_ref, slot)

    m_ref[...] = jnp.full(m_ref.shape, NEG_INF, F32)
    acc_ref[...] = jnp.zeros(acc_ref.shape, F32)
    score_fn = scorer(i, q_ref, slot)

    @pl.when(last == 0)
    def _():
        absorb(last, diag_buf)
        start_next()

    @pl.when(last > 0)
    def _():
        issue(score_fn, last - 1, "near", 1)
        absorb(last, diag_buf)
        n_far = last - 1

        def pair(n, carry):
            c = last - 2 - 2 * n
            issue(score_fn, c, "far", 0)
            absorb(c + 1, 1)
            issue(score_fn, c - 1, "far", 1)
            absorb(c, 0)
            return carry

        lax.fori_loop(0, n_far // 2, pair, 0)

        @pl.when(n_far % 2 == 1)
        def _():
            issue(score_fn, 0, "far", 0)
            absorb(1, 1)
            start_next()
            absorb(0, 0)

        @pl.when(n_far % 2 == 0)
        def _():
            start_next()
            absorb(0, 1)

    outs = [acc_ref[h, :HEAD_DIM] / acc_ref[h, HEAD_DIM:HEAD_DIM + 1] for h in heads]
    o_ref[0] = jnp.concatenate(outs, axis=0).T.astype(o_ref.dtype)


def _fox_body(q_ref, qnext_ref, k_ref, cp_ref, vt_ref, o_ref, mask_ref, *flash_refs):
    i = pl.program_id(2)

    @pl.when(i == 0)
    def _():
        r = lax.broadcasted_iota(jnp.int32, (KV_CHUNK, ATTN_TILE), 0)
        t = lax.broadcasted_iota(jnp.int32, (KV_CHUNK, ATTN_TILE), 1)
        for v in range(BLOCKS_PER_CHUNK):
            mask_ref[v] = jnp.where(r <= t + v * ATTN_TILE, 0.0, NEG_INF)

    def scorer(t, q_tile_ref, slot):
        q = q_tile_ref[0]
        lane = lax.broadcasted_iota(jnp.int32, (ATTN_TILE, LANES), 1)
        qxs = []
        for h in range(HEADS_PER_STEP):
            on_terms = (lane >= h * DECAY_PARTS) & (lane < (h + 1) * DECAY_PARTS)
            minus_one = jnp.where(on_terms, -1.0, 0.0).astype(BF16)
            qxs.append(jnp.concatenate([_head_only(q, h), minus_one], axis=1))

        def score_fn(c, kind):
            rows = _chunk_rows(c)
            k = k_ref[0, rows, :]
            terms = cp_ref[0, 0, rows, :]
            scores = [_dot_nt(jnp.concatenate([_lane_tile(k, h), terms], axis=1), qx)
                      for h, qx in enumerate(qxs)]
            if kind == "diag":
                mask = mask_ref[pl.ds(t % BLOCKS_PER_CHUNK, 1)][0]
                scores = [s + mask for s in scores]
            return scores

        return score_fn

    _flash_tile(i, pl.num_programs(2), lambda t, q_tile_ref, slot: None, scorer, q_ref, qnext_ref,
                vt_ref, o_ref, *flash_refs)


def _moba_body(q_ref, qnext_ref, k_ref, vt_ref, bias_ref, o_ref, kbar_ref, sel_ref, far_ref,
               *flash_refs):
    i = pl.program_id(2)
    n_blk = kbar_ref.shape[0]
    heads = range(HEADS_PER_STEP)

    @pl.when(i == 0)
    def _():
        for n in range(n_blk):
            kb = k_ref[0, n * MOBA_BLOCK:(n + 1) * MOBA_BLOCK, :].astype(F32)
            kbar_ref[n:n + 1, :] = jnp.mean(kb, axis=0, keepdims=True)

    def prepare(t, q_tile_ref, slot):
        qf = q_tile_ref[0].astype(F32)
        kbar = kbar_ref[...]
        blk = lax.broadcasted_iota(jnp.int32, (n_blk, ATTN_TILE), 0)
        for h in heads:
            gate = lax.dot_general(_head_only(kbar, h), _lane_tile(qf, h),
                                   (((1,), (1,)), ((), ())), preferred_element_type=F32,
                                   precision=lax.Precision.HIGHEST)
            gate = jnp.where(blk < t, gate, NEG_INF)
            keep = blk == t
            for _ in range(MOBA_TOPK):
                top = jnp.max(gate, axis=0, keepdims=True)
                first = jnp.min(jnp.where(gate == top, blk, n_blk), axis=0, keepdims=True)
                hit = blk == first
                keep = keep | (hit & (blk < t))
                gate = jnp.where(hit, -jnp.inf, gate)
            sel = jnp.where(keep, 0.0, NEG_INF)
            sel_ref[slot, h] = sel
            far_ref[slot, h] = sel + bias_ref[h, N_BIAS_SLOTS - 1, 0:1, 0:1]

    def scorer(t, q_tile_ref, slot):
        q = q_tile_ref[0]
        qms = [_head_only(q, h) for h in heads]

        def score_fn(c, kind):
            k = k_ref[0, _chunk_rows(c), :]
            out = []
            for h, qm in enumerate(qms):
                s = _dot_nt(_lane_tile(k, h), qm)
                parts = []
                for n in range(BLOCKS_PER_CHUNK):
                    j = c * BLOCKS_PER_CHUNK + n
                    tile = s[n * ATTN_TILE:(n + 1) * ATTN_TILE]
                    if kind == "far":
                        parts.append(tile + far_ref[slot, h, pl.ds(j, 1), :])
                    else:
                        bias = bias_ref[h, pl.ds(jnp.clip(t - j, 0, N_BIAS_SLOTS - 1), 1)][0]
                        parts.append(tile + bias + sel_ref[slot, h, pl.ds(j, 1), :])
                out.append(jnp.concatenate(parts, axis=0))
            return out

        return score_fn

    _flash_tile(i, pl.num_programs(2), prepare, scorer, q_ref, qnext_ref, vt_ref, o_ref,
                *flash_refs)


def _attn_specs(bsz, s):
    n_blk = s // ATTN_TILE
    q_spec = pl.BlockSpec((1, ATTN_TILE, STEP_LANES), lambda b, p, i: (b, i, p))
    once = pl.Buffered(1)
    k_spec = pl.BlockSpec((1, s, STEP_LANES), lambda b, p, i: (b, 0, p), pipeline_mode=once)
    vt_spec = pl.BlockSpec((1, s // KV_CHUNK, STEP_LANES, KV_CHUNK), lambda b, p, i: (b, 0, p, 0),
                           pipeline_mode=once)
    out_shape = jax.ShapeDtypeStruct((bsz, s, W_HEADS), BF16)
    grid = (bsz, W_HEADS // STEP_LANES, n_blk)
    return grid, q_spec, k_spec, vt_spec, out_shape


def _fox_attention(q, k, decay_terms, vt):
    bsz, s, _ = q.shape
    grid, q_spec, k_spec, vt_spec, out_shape = _attn_specs(bsz, s)
    return pl.pallas_call(
        _fox_body,
        grid=grid,
        in_specs=[q_spec, _next_tile_spec(grid[2]), k_spec,
                  pl.BlockSpec((1, 1, s, LANES), lambda b, p, i: (b, p, 0, 0),
                               pipeline_mode=pl.Buffered(1)),
                  vt_spec],
        out_specs=q_spec,
        out_shape=out_shape,
        scratch_shapes=[pltpu.VMEM((BLOCKS_PER_CHUNK, KV_CHUNK, ATTN_TILE), F32)]
        + _flash_scratch(),
        compiler_params=_params("parallel", "parallel", "arbitrary"),
        name="fox_attention",
    )(q, q, k, decay_terms, vt)


def _moba_attention(q, k, vt, bias):
    bsz, s, _ = q.shape
    grid, q_spec, k_spec, vt_spec, out_shape = _attn_specs(bsz, s)
    n_blk = s // MOBA_BLOCK
    return pl.pallas_call(
        _moba_body,
        grid=grid,
        in_specs=[q_spec, _next_tile_spec(grid[2]), k_spec, vt_spec,
                  pl.BlockSpec((HEADS_PER_STEP, N_BIAS_SLOTS, MOBA_BLOCK, MOBA_BLOCK),
                               lambda b, p, i: (p, 0, 0, 0), pipeline_mode=pl.Buffered(1))],
        out_specs=q_spec,
        out_shape=out_shape,
        scratch_shapes=[pltpu.VMEM((n_blk, STEP_LANES), F32),
                        pltpu.VMEM((2, HEADS_PER_STEP, n_blk, ATTN_TILE), F32),
                        pltpu.VMEM((2, HEADS_PER_STEP, n_blk, ATTN_TILE), F32)] + _flash_scratch(),
        compiler_params=_params("parallel", "parallel", "arbitrary"),
        name="moba_attention",
    )(q, q, k, vt, bias)


def _sgu_body(x_ref, win_ref, bin_ref, lg_ref, lb_ref, ws_ref, bs_ref, wout_ref, g_ref, b_ref,
              o_ref, y_ref):
    x = x_ref[...]
    xb = x.astype(BF16)
    w = x.shape[1]
    u = jax.nn.gelu(_dot(xb, win_ref[:, :w]) + bin_ref[:, :w])
    v = jax.nn.gelu(_dot(xb, win_ref[:, w:]) + bin_ref[:, w:])
    vb = _layer_norm(v, lg_ref[...], lb_ref[...]).astype(BF16)
    t = lax.broadcasted_iota(jnp.int32, (SGU_CHUNK, SGU_CHUNK), 0)
    s = lax.broadcasted_iota(jnp.int32, (SGU_CHUNK, SGU_CHUNK), 1)
    gw = w // SGU_GROUPS
    for g in range(SGU_GROUPS):
        w_g = jnp.where(t >= s, ws_ref[g], 0.0).astype(BF16)
        cols = slice(g * gw, (g + 1) * gw)
        for c in range(x.shape[0] // SGU_CHUNK):
            rows = slice(c * SGU_CHUNK, (c + 1) * SGU_CHUNK)
            mixed = _dot(w_g, vb[rows, cols]) + bs_ref[:, g:g + 1]
            y_ref[rows, cols] = (u[rows, cols] * mixed).astype(BF16)
    out = _dot(y_ref[...], wout_ref[...])
    o_ref[...] = _layer_norm(DN_ALPHA * x + out, g_ref[...], b_ref[...])


def _sgu(x, w_in, b_in, ln_g, ln_b, w_s, b_s_t, w_out, g, b):
    m, d = x.shape
    tm = min(ROW_TILE, m)
    row = pl.BlockSpec((tm, d), lambda i: (i, 0))
    weights = [w_in, b_in, ln_g, ln_b, w_s, b_s_t, w_out, g, b]
    return pl.pallas_call(
        _sgu_body,
        grid=(m // tm,),
        in_specs=[row] + [_resident(a.shape) for a in weights],
        out_specs=row,
        out_shape=jax.ShapeDtypeStruct((m, d), F32),
        scratch_shapes=[pltpu.VMEM((tm, d), BF16)],
        compiler_params=_params("parallel"),
        name="sgu",
    )(x, *weights)


def _mixer_attention(x2, bsz, s, w_in, b_f, rel_bias):
    d = x2.shape[1]
    seg = lambda n: w_in[:, n * W_HEADS:(n + 1) * W_HEADS]
    w_rows = jnp.concatenate([seg(0), seg(1), seg(3), seg(4)], axis=1).astype(BF16)
    w_t = jnp.concatenate([seg(2), seg(5), w_in[:, 6 * W_HEADS:]], axis=1).T.astype(BF16)
    qa, ka, qb, kb, vta, vtb, f_t = _attn_proj(x2.reshape(bsz, s, d), w_rows, w_t)
    n_groups = N_HEADS_B // HEADS_PER_STEP
    terms = _fox_decay(f_t, b_f).reshape(bsz, DECAY_PARTS, n_groups, HEADS_PER_STEP, s)
    terms = terms.transpose(0, 2, 4, 3, 1).reshape(bsz, n_groups, s, HEADS_PER_STEP * DECAY_PARTS)
    terms = jnp.pad(terms.astype(BF16), ((0, 0),) * 3 + ((0, LANES - terms.shape[-1]),))
    o_a = _moba_attention(qa, ka, vta, _moba_bias(rel_bias))
    o_b = _fox_attention(qb, kb, terms, vtb)
    return o_a.reshape(bsz * s, W_HEADS), o_b.reshape(bsz * s, W_HEADS)


def kernel(x, p, ln_g, ln_b, ffn_w_gate, ffn_w_up, ffn_w_down, attn_w_in, attn_b_f, attn_w_out,
           rel_bias, sgu_w_in, sgu_b_in, sgu_ln_g, sgu_ln_b, sgu_w_s, sgu_b_s, sgu_w_out,
           ple_w_proj, ple_w_gate):
    bsz, s, d = x.shape
    x2 = x.reshape(bsz * s, d)
    vec = lambda a: a.reshape(1, -1)
    wg, wu, wd = ffn_w_gate.astype(BF16), ffn_w_up.astype(BF16), ffn_w_down.astype(BF16)
    ln_g4, ln_b4 = ln_g.reshape(DEPTH, 3, 1, d), ln_b.reshape(DEPTH, 3, 1, d)
    p3 = p.reshape(DEPTH, bsz * s, p.shape[-1])
    w_ple_gate, w_ple_proj = ple_w_gate.astype(BF16), ple_w_proj.astype(BF16)
    w_attn_out = attn_w_out.astype(BF16).reshape(attn_w_out.shape[0], 2, W_HEADS, d)
    for i in range(DEPTH):
        j = i // 2
        ln = lambda n: [_pick(ln_g4, i, n), _pick(ln_b4, i, n)]
        ffn_w = lambda n: [_pick(wg, i, n), _pick(wu, i, n), _pick(wd, i, n)] + ln(2 * n)
        x2 = _ffn(x2, ffn_w(0))
        ple = ((p3, i), _pick(w_ple_gate, i), _pick(w_ple_proj, i))
        if i % 2 == 0:
            o_a, o_b = _mixer_attention(x2, bsz, s, attn_w_in[j], attn_b_f[j], rel_bias)
            mix = [o_a, o_b, _pick(w_attn_out, j, 0), _pick(w_attn_out, j, 1)] + ln(1)
            x2 = _ffn(x2, ffn_w(1), mix=mix, ple=ple)
        else:
            x2 = _sgu(x2, sgu_w_in[j].astype(BF16), vec(sgu_b_in[j]), vec(sgu_ln_g[j]),
                      vec(sgu_ln_b[j]), sgu_w_s[j], sgu_b_s[j].T, sgu_w_out[j].astype(BF16),
                      vec(ln_g[i, 1]), vec(ln_b[i, 1]))
            x2 = _ffn(x2, ffn_w(1), ple=ple)
    return x2.reshape(bsz, s, d)
```

```python
import functools
import math

import jax
import jax.numpy as jnp
from jax import lax
from jax.experimental import pallas as pl
from jax.experimental.pallas import tpu as pltpu

D_MODEL = 1024
HEAD_DIM = 64
N_HEADS_A = 8
N_HEADS_B = 8
W_HEADS = N_HEADS_A * HEAD_DIM
MOBA_BLOCK = 256
MOBA_TOPK = 3
REL_BUCKETS = 32
REL_MAX_DIST = 128
SGU_CHUNK = 128
SGU_GROUPS = 8
D_FF = 2816
PLE_DIM = 256
DEPTH = 2
DN_ALPHA = (2.0 * DEPTH) ** 0.25
LN_EPS = 1e-5
NEG_INF = -1e30
LOG2_E = math.log2(math.e)

F32 = jnp.float32
BF16 = jnp.bfloat16

VMEM_LIMIT_BYTES = 56 * 1024 * 1024
LANES = 128
HEADS_PER_TILE = LANES // HEAD_DIM
HEADS_PER_STEP = 8
STEP_LANES = HEADS_PER_STEP * HEAD_DIM
FF_CHUNK = 256
ROW_TILE = 512
ATTN_TILE = MOBA_BLOCK
KV_CHUNK = 512
BLOCKS_PER_CHUNK = KV_CHUNK // ATTN_TILE
N_BIAS_SLOTS = 3
DECAY_PARTS = 3
GATE_TERMS = 3
ONES_ROWS = 16


def _params(*semantics):
    return pltpu.CompilerParams(dimension_semantics=semantics, vmem_limit_bytes=VMEM_LIMIT_BYTES)


def _resident(shape):
    zeros = (0,) * len(shape)
    return pl.BlockSpec(shape, lambda *_: zeros, pipeline_mode=pl.Buffered(1))


def _pick(array, *lead):
    tail = array.shape[len(lead):]
    index = tuple(lead) + (0,) * len(tail)
    spec = pl.BlockSpec((None,) * len(lead) + tail, lambda *_: index, pipeline_mode=pl.Buffered(1))
    return array, spec


def _layer_norm(y, g, b):
    mu = jnp.mean(y, axis=-1, keepdims=True)
    d = y - mu
    var = jnp.mean(d * d, axis=-1, keepdims=True)
    return d * lax.rsqrt(var + LN_EPS) * g + b


def _dot(a, b):
    return jnp.dot(a, b, preferred_element_type=F32)


def _dot_nt(a, b):
    return lax.dot_general(a, b, (((1,), (1,)), ((), ())), preferred_element_type=F32)


def _ffn_body(x_ref, *refs, with_mix, with_ple):
    refs = list(refs)
    o_ref = refs.pop()
    x = x_ref[...]
    if with_mix:
        oa_ref, ob_ref, wa_ref, wb_ref, mg_ref, mb_ref = refs[:6]
        refs = refs[6:]
        mixed = _dot(oa_ref[...], wa_ref[...]) + _dot(ob_ref[...], wb_ref[...])
        x = _layer_norm(DN_ALPHA * x + mixed, mg_ref[...], mb_ref[...])
    wg_ref, wu_ref, wd_ref, g_ref, b_ref = refs[:5]
    if with_ple:
        p_ref, wpg_ref, wpp_ref = refs[5:]
    xb = x.astype(BF16)
    acc = jnp.zeros(x.shape, F32)
    for c in range(D_FF // FF_CHUNK):
        sl = slice(c * FF_CHUNK, (c + 1) * FF_CHUNK)
        gate = _dot(xb, wg_ref[:, sl])
        up = _dot(xb, wu_ref[:, sl])
        h = (jax.nn.silu(gate) * up).astype(BF16)
        acc = acc + _dot(h, wd_ref[sl, :])
    y = _layer_norm(DN_ALPHA * x + 0.5 * acc, g_ref[...], b_ref[...])
    if with_ple:
        gate = jax.nn.sigmoid(_dot(y.astype(BF16), wpg_ref[...]))
        y = y + gate * _dot(p_ref[...].astype(BF16), wpp_ref[...])
    o_ref[...] = y


def _ffn(x, weights, mix=None, ple=None):
    m, d = x.shape
    tm = min(ROW_TILE, m)
    row = lambda w: pl.BlockSpec((tm, w), lambda i: (i, 0))
    args, in_specs = [x], [row(d)]

    def add(picks):
        for array, spec in picks:
            args.append(array)
            in_specs.append(spec)

    if mix is not None:
        o_a, o_b = mix[:2]
        add([(o_a, row(o_a.shape[1])), (o_b, row(o_b.shape[1]))])
        add(mix[2:])
    add(weights)
    if ple is not None:
        (p, layer), wpg, wpp = ple
        add([(p, pl.BlockSpec((None, tm, p.shape[2]), lambda i: (layer, i, 0))), wpg, wpp])
    return pl.pallas_call(
        functools.partial(_ffn_body, with_mix=mix is not None, with_ple=ple is not None),
        grid=(m // tm,),
        in_specs=in_specs,
        out_specs=row(d),
        out_shape=jax.ShapeDtypeStruct((m, d), F32),
        compiler_params=_params("parallel"),
        name="ffn" + ("_mix" if mix is not None else "") + ("_ple" if ple is not None else ""),
    )(*args)


def _proj_body(x_ref, w_ref, wt_ref, qa_ref, ka_ref, qb_ref, kb_ref, vta_ref, vtb_ref, f_ref):
    xb = x_ref[0].astype(BF16)
    scale = HEAD_DIM ** -0.5 * LOG2_E

    def seg(n):
        return _dot(xb, w_ref[:, n * W_HEADS:(n + 1) * W_HEADS])

    qa_ref[0] = (seg(0) * scale).astype(BF16)
    ka_ref[0] = seg(1).astype(BF16)
    qb_ref[0] = (seg(2) * scale).astype(BF16)
    kb_ref[0] = seg(3).astype(BF16)
    rt = _dot_nt(wt_ref[...], xb)
    for n in range(vta_ref.shape[1]):
        cols = slice(n * KV_CHUNK, (n + 1) * KV_CHUNK)
        vta_ref[0, n] = rt[:W_HEADS, cols].astype(BF16)
        vtb_ref[0, n] = rt[W_HEADS:2 * W_HEADS, cols].astype(BF16)
    f_ref[0] = rt[2 * W_HEADS:, :]


def _attn_proj(x, w_rows, w_t):
    bsz, s, d = x.shape
    tm = min(ROW_TILE, s)
    n_chunk = tm // KV_CHUNK
    rows = pl.BlockSpec((1, tm, W_HEADS), lambda b, i: (b, i, 0))
    vts = pl.BlockSpec((1, n_chunk, W_HEADS, KV_CHUNK), lambda b, i: (b, i, 0, 0))
    row_t = jax.ShapeDtypeStruct((bsz, s, W_HEADS), BF16)
    vt_t = jax.ShapeDtypeStruct((bsz, s // KV_CHUNK, W_HEADS, KV_CHUNK), BF16)
    return pl.pallas_call(
        _proj_body,
        grid=(bsz, s // tm),
        in_specs=[pl.BlockSpec((1, tm, d), lambda b, i: (b, i, 0)),
                  _resident(w_rows.shape), _resident(w_t.shape)],
        out_specs=[rows, rows, rows, rows, vts, vts,
                   pl.BlockSpec((1, N_HEADS_B, tm), lambda b, i: (b, 0, i))],
        out_shape=[row_t, row_t, row_t, row_t, vt_t, vt_t,
                   jax.ShapeDtypeStruct((bsz, N_HEADS_B, s), F32)],
        compiler_params=_params("parallel", "parallel"),
        name="attn_proj",
    )(x, w_rows, w_t)


def _decay_body(f_ref, bf_ref, c_ref):
    x = f_ref[0] + bf_ref[...]
    log_f = jnp.minimum(x, 0.0) - jnp.log(1.0 + jnp.exp(-jnp.abs(x)))
    r = lax.broadcasted_iota(jnp.int32, (LANES, LANES), 0)
    c = lax.broadcasted_iota(jnp.int32, (LANES, LANES), 1)
    prefix = (r <= c).astype(BF16)
    carry = jnp.zeros((x.shape[0], 1), F32)
    for n in range(x.shape[1] // LANES):
        sl = slice(n * LANES, (n + 1) * LANES)
        rest = log_f[:, sl]
        within = jnp.zeros(rest.shape, F32)
        for _ in range(DECAY_PARTS):
            part = rest.astype(BF16)
            rest = rest - part.astype(F32)
            within = within + _dot(part, prefix)
        cs = within + carry
        carry = cs[:, LANES - 1:LANES]
        rest = cs * LOG2_E
        for part in range(DECAY_PARTS):
            term = rest.astype(BF16).astype(F32)
            c_ref[0, part, :, sl] = term
            rest = rest - term


def _fox_decay(f_t, b_f):
    bsz, h, s = f_t.shape
    return pl.pallas_call(
        _decay_body,
        grid=(bsz,),
        in_specs=[pl.BlockSpec((1, h, s), lambda b: (b, 0, 0)), _resident((h, 1))],
        out_specs=pl.BlockSpec((1, DECAY_PARTS, h, s), lambda b: (b, 0, 0, 0)),
        out_shape=jax.ShapeDtypeStruct((bsz, DECAY_PARTS, h, s), F32),
        compiler_params=_params("parallel"),
        name="fox_decay",
    )(f_t, b_f.reshape(h, 1))


def _bias_body(tbl_ref, o_ref):
    h = pl.program_id(0)
    r = lax.broadcasted_iota(jnp.int32, (MOBA_BLOCK, MOBA_BLOCK), 0)
    t = lax.broadcasted_iota(jnp.int32, (MOBA_BLOCK, MOBA_BLOCK), 1)
    max_exact = REL_BUCKETS // 2
    for slot in range(N_BIAS_SLOTS):
        n = jnp.maximum(t - r + slot * MOBA_BLOCK, 0)
        nf = jnp.maximum(n, max_exact).astype(F32)
        large = max_exact + (jnp.log(nf / max_exact) / math.log(REL_MAX_DIST / max_exact)
                             * (REL_BUCKETS - max_exact)).astype(jnp.int32)
        large = jnp.minimum(large, REL_BUCKETS - 1)
        bucket = jnp.where(n < max_exact, n, large)
        bias = jnp.zeros((MOBA_BLOCK, MOBA_BLOCK), F32)
        for k in range(REL_BUCKETS):
            bias = jnp.where(bucket == k, tbl_ref[k, h], bias)
        bias = bias * LOG2_E
        if slot == 0:
            bias = jnp.where(t >= r, bias, NEG_INF)
        o_ref[0, slot] = bias


def _moba_bias(rel_bias):
    shape = (N_HEADS_A, N_BIAS_SLOTS, MOBA_BLOCK, MOBA_BLOCK)
    return pl.pallas_call(
        _bias_body,
        grid=(N_HEADS_A,),
        in_specs=[pl.BlockSpec(memory_space=pltpu.SMEM)],
        out_specs=pl.BlockSpec((1,) + shape[1:], lambda h: (h, 0, 0, 0)),
        out_shape=jax.ShapeDtypeStruct(shape, F32),
        compiler_params=_params("parallel"),
        name="moba_bias",
    )(rel_bias)


def _lane_tile(x, head):
    tile = head // HEADS_PER_TILE
    return x[:, tile * LANES:(tile + 1) * LANES]


def _head_only(x, head):
    xt = _lane_tile(x, head)
    lane = lax.broadcasted_iota(jnp.int32, xt.shape, 1)
    return jnp.where(lane // HEAD_DIM == head % HEADS_PER_TILE, xt, jnp.zeros_like(xt))


def _chunk_rows(c):
    return pl.ds(pl.multiple_of(c * KV_CHUNK, KV_CHUNK), KV_CHUNK)


def _flash_scratch():
    score_buf = pltpu.VMEM((HEADS_PER_STEP, KV_CHUNK, ATTN_TILE), F32)
    max_buf = pltpu.VMEM((HEADS_PER_STEP, 1, ATTN_TILE), F32)
    return [score_buf] * 3 + [max_buf] * 3 + [
        pltpu.VMEM((HEADS_PER_STEP, 1, ATTN_TILE), F32),
        pltpu.VMEM((HEADS_PER_STEP, HEAD_DIM + ONES_ROWS, ATTN_TILE), F32)]


def _next_tile_spec(n_tiles):
    return pl.BlockSpec((1, ATTN_TILE, STEP_LANES),
                        lambda b, p, i: (b, jnp.minimum(i + 1, n_tiles - 1), p))


def _flash_tile(i, n_tiles, prepare, scorer, q_ref, qnext_ref, vt_ref, o_ref,
                s0_ref, s1_ref, sd_ref, mx0_ref, mx1_ref, mxd_ref, m_ref, acc_ref):
    heads = range(HEADS_PER_STEP)
    diag_buf = 2
    bufs = ((s0_ref, mx0_ref), (s1_ref, mx1_ref), (sd_ref, mxd_ref))
    last = i // BLOCKS_PER_CHUNK
    slot = i % 2
    ones = jnp.ones((ONES_ROWS, KV_CHUNK), BF16)

    def issue(score_fn, c, kind, buf):
        s_ref, mx_ref = bufs[buf]
        for h, s in enumerate(score_fn(c, kind)):
            s_ref[h] = s
            mx_ref[h] = jnp.max(s, axis=0, keepdims=True)

    def absorb(c, buf):
        s_ref, mx_ref = bufs[buf]
        for h in heads:
            m = m_ref[h]
            m_new = jnp.maximum(m, mx_ref[h])
            p = jnp.exp2(s_ref[h] - m_new).astype(BF16)
            vt = jnp.concatenate([vt_ref[0, c, h * HEAD_DIM:(h + 1) * HEAD_DIM, :], ones], axis=0)
            acc_ref[h] = jnp.exp2(m - m_new) * acc_ref[h] + _dot(vt, p)
            m_ref[h] = m_new

    def start_tile(t, q_tile_ref, t_slot):
        prepare(t, q_tile_ref, t_slot)
        issue(scorer(t, q_tile_ref, t_slot), t // BLOCKS_PER_CHUNK, "diag", diag_buf)

    def start_next():
        start_tile(jnp.minimum(i + 1, n_tiles - 1), qnext_ref, 1 - slot)

    @pl.when(i == 0)
    def _():
        start_tile(i, q_ref, slot)

    m_ref[...] = jnp.full(m_ref.shape, NEG_INF, F32)
    acc_ref[...] = jnp.zeros(acc_ref.shape, F32)
    score_fn = scorer(i, q_ref, slot)

    @pl.when(last == 0)
    def _():
        absorb(last, diag_buf)
        start_next()

    @pl.when(last > 0)
    def _():
        issue(score_fn, last - 1, "near", 1)
        absorb(last, diag_buf)
        n_far = last - 1

        def pair(n, carry):
            c = last - 2 - 2 * n
            issue(score_fn, c, "far", 0)
            absorb(c + 1, 1)
            issue(score_fn, c - 1, "far", 1)
            absorb(c, 0)
            return carry

        lax.fori_loop(0, n_far // 2, pair, 0)

        @pl.when(n_far % 2 == 1)
        def _():
            issue(score_fn, 0, "far", 0)
            absorb(1, 1)
            start_next()
            absorb(0, 0)

        @pl.when(n_far % 2 == 0)
        def _():
            start_next()
            absorb(0, 1)

    outs = [acc_ref[h, :HEAD_DIM] / acc_ref[h, HEAD_DIM:HEAD_DIM + 1] for h in heads]
    o_ref[0] = jnp.concatenate(outs, axis=0).T.astype(o_ref.dtype)


def _fox_body(q_ref, qnext_ref, k_ref, cp_ref, vt_ref, o_ref, mask_ref, *flash_refs):
    i = pl.program_id(2)

    @pl.when(i == 0)
    def _():
        r = lax.broadcasted_iota(jnp.int32, (KV_CHUNK, ATTN_TILE), 0)
        t = lax.broadcasted_iota(jnp.int32, (KV_CHUNK, ATTN_TILE), 1)
        for v in range(BLOCKS_PER_CHUNK):
            mask_ref[v] = jnp.where(r <= t + v * ATTN_TILE, 0.0, NEG_INF)

    def scorer(t, q_tile_ref, slot):
        q = q_tile_ref[0]
        lane = lax.broadcasted_iota(jnp.int32, (ATTN_TILE, LANES), 1)
        qxs = []
        for h in range(HEADS_PER_STEP):
            on_terms = (lane >= h * DECAY_PARTS) & (lane < (h + 1) * DECAY_PARTS)
            minus_one = jnp.where(on_terms, -1.0, 0.0).astype(BF16)
            qxs.append(jnp.concatenate([_head_only(q, h), minus_one], axis=1))

        def score_fn(c, kind):
            rows = _chunk_rows(c)
            k = k_ref[0, rows, :]
            terms = cp_ref[0, 0, rows, :]
            scores = [_dot_nt(jnp.concatenate([_lane_tile(k, h), terms], axis=1), qx)
                      for h, qx in enumerate(qxs)]
            if kind == "diag":
                mask = mask_ref[pl.ds(t % BLOCKS_PER_CHUNK, 1)][0]
                scores = [s + mask for s in scores]
            return scores

        return score_fn

    _flash_tile(i, pl.num_programs(2), lambda t, q_tile_ref, slot: None, scorer, q_ref, qnext_ref,
                vt_ref, o_ref, *flash_refs)


def _moba_body(q_ref, qnext_ref, k_ref, vt_ref, bias_ref, o_ref, kbar_ref, kterms_ref, sel_ref,
               far_ref, *flash_refs):
    i = pl.program_id(2)
    n_blk = kbar_ref.shape[0]
    heads = range(HEADS_PER_STEP)

    @pl.when(i == 0)
    def _():
        for n in range(n_blk):
            kb = k_ref[0, n * MOBA_BLOCK:(n + 1) * MOBA_BLOCK, :].astype(F32)
            kbar_ref[n:n + 1, :] = jnp.mean(kb, axis=0, keepdims=True)
        rest = kbar_ref[...]
        lane = lax.broadcasted_iota(jnp.int32, (n_blk, LANES), 1)
        for term in range(GATE_TERMS):
            part = rest.astype(BF16)
            rest = rest - part.astype(F32)
            for tile in range(STEP_LANES // LANES):
                part_tile = part[:, tile * LANES:(tile + 1) * LANES]
                for h in range(HEADS_PER_TILE):
                    row = (h * GATE_TERMS + term) * n_blk
                    kterms_ref[tile, row:row + n_blk, :] = jnp.where(
                        lane // HEAD_DIM == h, part_tile, jnp.zeros_like(part_tile))

    def prepare(t, q_tile_ref, slot):
        q = q_tile_ref[0]
        blk = lax.broadcasted_iota(jnp.int32, (n_blk, ATTN_TILE), 0)
        for h in heads:
            if h % HEADS_PER_TILE == 0:
                tile_gates = _dot_nt(kterms_ref[h // HEADS_PER_TILE], _lane_tile(q, h))
            row = (h % HEADS_PER_TILE) * GATE_TERMS * n_blk
            gate = tile_gates[row:row + n_blk]
            for term in range(1, GATE_TERMS):
                gate = gate + tile_gates[row + term * n_blk:row + (term + 1) * n_blk]
            gate = jnp.where(blk < t, gate, NEG_INF)
            keep = blk == t
            for _ in range(MOBA_TOPK):
                top = jnp.max(gate, axis=0, keepdims=True)
                first = jnp.min(jnp.where(gate == top, blk, n_blk), axis=0, keepdims=True)
                hit = blk == first
                keep = keep | (hit & (blk < t))
                gate = jnp.where(hit, -jnp.inf, gate)
            sel = jnp.where(keep, 0.0, NEG_INF)
            sel_ref[slot, h] = sel
            far_ref[slot, h] = sel + bias_ref[h, N_BIAS_SLOTS - 1, 0:1, 0:1]

    def scorer(t, q_tile_ref, slot):
        q = q_tile_ref[0]
        qms = [_head_only(q, h) for h in heads]

        def score_fn(c, kind):
            k = k_ref[0, _chunk_rows(c), :]
            out = []
            for h, qm in enumerate(qms):
                s = _dot_nt(_lane_tile(k, h), qm)
                parts = []
                for n in range(BLOCKS_PER_CHUNK):
                    j = c * BLOCKS_PER_CHUNK + n
                    tile = s[n * ATTN_TILE:(n + 1) * ATTN_TILE]
                    if kind == "far":
                        parts.append(tile + far_ref[slot, h, pl.ds(j, 1), :])
                    else:
                        bias = bias_ref[h, pl.ds(jnp.clip(t - j, 0, N_BIAS_SLOTS - 1), 1)][0]
                        parts.append(tile + bias + sel_ref[slot, h, pl.ds(j, 1), :])
                out.append(jnp.concatenate(parts, axis=0))
            return out

        return score_fn

    _flash_tile(i, pl.num_programs(2), prepare, scorer, q_ref, qnext_ref, vt_ref, o_ref,
                *flash_refs)


def _attn_specs(bsz, s):
    n_blk = s // ATTN_TILE
    q_spec = pl.BlockSpec((1, ATTN_TILE, STEP_LANES), lambda b, p, i: (b, i, p))
    once = pl.Buffered(1)
    k_spec = pl.BlockSpec((1, s, STEP_LANES), lambda b, p, i: (b, 0, p), pipeline_mode=once)
    vt_spec = pl.BlockSpec((1, s // KV_CHUNK, STEP_LANES, KV_CHUNK), lambda b, p, i: (b, 0, p, 0),
                           pipeline_mode=once)
    out_shape = jax.ShapeDtypeStruct((bsz, s, W_HEADS), BF16)
    grid = (bsz, W_HEADS // STEP_LANES, n_blk)
    return grid, q_spec, k_spec, vt_spec, out_shape


def _fox_attention(q, k, decay_terms, vt):
    bsz, s, _ = q.shape
    grid, q_spec, k_spec, vt_spec, out_shape = _attn_specs(bsz, s)
    return pl.pallas_call(
        _fox_body,
        grid=grid,
        in_specs=[q_spec, _next_tile_spec(grid[2]), k_spec,
                  pl.BlockSpec((1, 1, s, LANES), lambda b, p, i: (b, p, 0, 0),
                               pipeline_mode=pl.Buffered(1)),
                  vt_spec],
        out_specs=q_spec,
        out_shape=out_shape,
        scratch_shapes=[pltpu.VMEM((BLOCKS_PER_CHUNK, KV_CHUNK, ATTN_TILE), F32)]
        + _flash_scratch(),
        compiler_params=_params("parallel", "parallel", "arbitrary"),
        name="fox_attention",
    )(q, q, k, decay_terms, vt)


def _moba_attention(q, k, vt, bias):
    bsz, s, _ = q.shape
    grid, q_spec, k_spec, vt_spec, out_shape = _attn_specs(bsz, s)
    n_blk = s // MOBA_BLOCK
    return pl.pallas_call(
        _moba_body,
        grid=grid,
        in_specs=[q_spec, _next_tile_spec(grid[2]), k_spec, vt_spec,
                  pl.BlockSpec((HEADS_PER_STEP, N_BIAS_SLOTS, MOBA_BLOCK, MOBA_BLOCK),
                               lambda b, p, i: (p, 0, 0, 0), pipeline_mode=pl.Buffered(1))],
        out_specs=q_spec,
        out_shape=out_shape,
        scratch_shapes=[pltpu.VMEM((n_blk, STEP_LANES), F32),
                        pltpu.VMEM((STEP_LANES // LANES, HEADS_PER_TILE * GATE_TERMS * n_blk, LANES),
                                   BF16),
                        pltpu.VMEM((2, HEADS_PER_STEP, n_blk, ATTN_TILE), F32),
                        pltpu.VMEM((2, HEADS_PER_STEP, n_blk, ATTN_TILE), F32)] + _flash_scratch(),
        compiler_params=_params("parallel", "parallel", "arbitrary"),
        name="moba_attention",
    )(q, q, k, vt, bias)


def _sgu_body(x_ref, win_ref, bin_ref, lg_ref, lb_ref, ws_ref, bs_ref, wout_ref, g_ref, b_ref,
              o_ref, y_ref):
    x = x_ref[...]
    xb = x.astype(BF16)
    w = x.shape[1]
    u = jax.nn.gelu(_dot(xb, win_ref[:, :w]) + bin_ref[:, :w])
    v = jax.nn.gelu(_dot(xb, win_ref[:, w:]) + bin_ref[:, w:])
    vb = _layer_norm(v, lg_ref[...], lb_ref[...]).astype(BF16)
    t = lax.broadcasted_iota(jnp.int32, (SGU_CHUNK, SGU_CHUNK), 0)
    s = lax.broadcasted_iota(jnp.int32, (SGU_CHUNK, SGU_CHUNK), 1)
    gw = w // SGU_GROUPS
    for g in range(SGU_GROUPS):
        w_g = jnp.where(t >= s, ws_ref[g], 0.0).astype(BF16)
        cols = slice(g * gw, (g + 1) * gw)
        for c in range(x.shape[0] // SGU_CHUNK):
            rows = slice(c * SGU_CHUNK, (c + 1) * SGU_CHUNK)
            mixed = _dot(w_g, vb[rows, cols]) + bs_ref[:, g:g + 1]
            y_ref[rows, cols] = (u[rows, cols] * mixed).astype(BF16)
    out = _dot(y_ref[...], wout_ref[...])
    o_ref[...] = _layer_norm(DN_ALPHA * x + out, g_ref[...], b_ref[...])


def _sgu(x, w_in, b_in, ln_g, ln_b, w_s, b_s_t, w_out, g, b):
    m, d = x.shape
    tm = min(ROW_TILE, m)
    row = pl.BlockSpec((tm, d), lambda i: (i, 0))
    weights = [w_in, b_in, ln_g, ln_b, w_s, b_s_t, w_out, g, b]
    return pl.pallas_call(
        _sgu_body,
        grid=(m // tm,),
        in_specs=[row] + [_resident(a.shape) for a in weights],
        out_specs=row,
        out_shape=jax.ShapeDtypeStruct((m, d), F32),
        scratch_shapes=[pltpu.VMEM((tm, d), BF16)],
        compiler_params=_params("parallel"),
        name="sgu",
    )(x, *weights)


def _mixer_attention(x2, bsz, s, w_in, b_f, rel_bias):
    d = x2.shape[1]
    seg = lambda n: w_in[:, n * W_HEADS:(n + 1) * W_HEADS]
    w_rows = jnp.concatenate([seg(0), seg(1), seg(3), seg(4)], axis=1).astype(BF16)
    w_t = jnp.concatenate([seg(2), seg(5), w_in[:, 6 * W_HEADS:]], axis=1).T.astype(BF16)
    qa, ka, qb, kb, vta, vtb, f_t = _attn_proj(x2.reshape(bsz, s, d), w_rows, w_t)
    n_groups = N_HEADS_B // HEADS_PER_STEP
    terms = _fox_decay(f_t, b_f).reshape(bsz, DECAY_PARTS, n_groups, HEADS_PER_STEP, s)
    terms = terms.transpose(0, 2, 4, 3, 1).reshape(bsz, n_groups, s, HEADS_PER_STEP * DECAY_PARTS)
    terms = jnp.pad(terms.astype(BF16), ((0, 0),) * 3 + ((0, LANES - terms.shape[-1]),))
    o_a = _moba_attention(qa, ka, vta, _moba_bias(rel_bias))
    o_b = _fox_attention(qb, kb, terms, vtb)
    return o_a.reshape(bsz * s, W_HEADS), o_b.reshape(bsz * s, W_HEADS)


def kernel(x, p, ln_g, ln_b, ffn_w_gate, ffn_w_up, ffn_w_down, attn_w_in, attn_b_f, attn_w_out,
           rel_bias, sgu_w_in, sgu_b_in, sgu_ln_g, sgu_ln_b, sgu_w_s, sgu_b_s, sgu_w_out,
           ple_w_proj, ple_w_gate):
    bsz, s, d = x.shape
    x2 = x.reshape(bsz * s, d)
    vec = lambda a: a.reshape(1, -1)
    wg, wu, wd = ffn_w_gate.astype(BF16), ffn_w_up.astype(BF16), ffn_w_down.astype(BF16)
    ln_g4, ln_b4 = ln_g.reshape(DEPTH, 3, 1, d), ln_b.reshape(DEPTH, 3, 1, d)
    p3 = p.reshape(DEPTH, bsz * s, p.shape[-1])
    w_ple_gate, w_ple_proj = ple_w_gate.astype(BF16), ple_w_proj.astype(BF16)
    w_attn_out = attn_w_out.astype(BF16).reshape(attn_w_out.shape[0], 2, W_HEADS, d)
    for i in range(DEPTH):
        j = i // 2
        ln = lambda n: [_pick(ln_g4, i, n), _pick(ln_b4, i, n)]
        ffn_w = lambda n: [_pick(wg, i, n), _pick(wu, i, n), _pick(wd, i, n)] + ln(2 * n)
        x2 = _ffn(x2, ffn_w(0))
        ple = ((p3, i), _pick(w_ple_gate, i), _pick(w_ple_proj, i))
        if i % 2 == 0:
            o_a, o_b = _mixer_attention(x2, bsz, s, attn_w_in[j], attn_b_f[j], rel_bias)
            mix = [o_a, o_b, _pick(w_attn_out, j, 0), _pick(w_attn_out, j, 1)] + ln(1)
            x2 = _ffn(x2, ffn_w(1), mix=mix, ple=ple)
        else:
            x2 = _sgu(x2, sgu_w_in[j].astype(BF16), vec(sgu_b_in[j]), vec(sgu_ln_g[j]),
                      vec(sgu_ln_b[j]), sgu_w_s[j], sgu_b_s[j].T, sgu_w_out[j].astype(BF16),
                      vec(ln_g[i, 1]), vec(ln_b[i, 1]))
            x2 = _ffn(x2, ffn_w(1), ple=ple)
    return x2.reshape(bsz, s, d)
```

```python
import functools
import math

import jax
import jax.numpy as jnp
from jax import lax
from jax.experimental import pallas as pl
from jax.experimental.pallas import tpu as pltpu

D_MODEL = 1024
HEAD_DIM = 64
N_HEADS_A = 8
N_HEADS_B = 8
W_HEADS = N_HEADS_A * HEAD_DIM
MOBA_BLOCK = 256
MOBA_TOPK = 3
REL_BUCKETS = 32
REL_MAX_DIST = 128
SGU_CHUNK = 128
SGU_GROUPS = 8
D_FF = 2816
PLE_DIM = 256
DEPTH = 2
DN_ALPHA = (2.0 * DEPTH) ** 0.25
LN_EPS = 1e-5
NEG_INF = -1e30
LOG2_E = math.log2(math.e)

F32 = jnp.float32
BF16 = jnp.bfloat16

VMEM_LIMIT_BYTES = 56 * 1024 * 1024
LANES = 128
HEADS_PER_TILE = LANES // HEAD_DIM
HEADS_PER_STEP = 4
STEP_LANES = HEADS_PER_STEP * HEAD_DIM
FF_CHUNK = 256
ROW_TILE = 512
ATTN_TILE = MOBA_BLOCK
KV_CHUNK = 512
BLOCKS_PER_CHUNK = KV_CHUNK // ATTN_TILE
N_BIAS_SLOTS = 3
DECAY_PARTS = 3
GATE_TERMS = 3
ONES_ROWS = 16


def _params(*semantics):
    return pltpu.CompilerParams(dimension_semantics=semantics, vmem_limit_bytes=VMEM_LIMIT_BYTES)


def _resident(shape):
    zeros = (0,) * len(shape)
    return pl.BlockSpec(shape, lambda *_: zeros, pipeline_mode=pl.Buffered(1))


def _pick(array, *lead):
    tail = array.shape[len(lead):]
    index = tuple(lead) + (0,) * len(tail)
    spec = pl.BlockSpec((None,) * len(lead) + tail, lambda *_: index, pipeline_mode=pl.Buffered(1))
    return array, spec


def _layer_norm(y, g, b):
    mu = jnp.mean(y, axis=-1, keepdims=True)
    d = y - mu
    var = jnp.mean(d * d, axis=-1, keepdims=True)
    return d * lax.rsqrt(var + LN_EPS) * g + b


def _dot(a, b):
    return jnp.dot(a, b, preferred_element_type=F32)


def _dot_nt(a, b):
    return lax.dot_general(a, b, (((1,), (1,)), ((), ())), preferred_element_type=F32)


def _ffn_body(x_ref, *refs, with_mix, with_ple):
    refs = list(refs)
    o_ref = refs.pop()
    x = x_ref[...]
    if with_mix:
        oa_ref, ob_ref, wa_ref, wb_ref, mg_ref, mb_ref = refs[:6]
        refs = refs[6:]
        mixed = _dot(oa_ref[...], wa_ref[...]) + _dot(ob_ref[...], wb_ref[...])
        x = _layer_norm(DN_ALPHA * x + mixed, mg_ref[...], mb_ref[...])
    wg_ref, wu_ref, wd_ref, g_ref, b_ref = refs[:5]
    if with_ple:
        p_ref, wpg_ref, wpp_ref = refs[5:]
    xb = x.astype(BF16)
    acc = jnp.zeros(x.shape, F32)
    for c in range(D_FF // FF_CHUNK):
        sl = slice(c * FF_CHUNK, (c + 1) * FF_CHUNK)
        gate = _dot(xb, wg_ref[:, sl])
        up = _dot(xb, wu_ref[:, sl])
        h = (jax.nn.silu(gate) * up).astype(BF16)
        acc = acc + _dot(h, wd_ref[sl, :])
    y = _layer_norm(DN_ALPHA * x + 0.5 * acc, g_ref[...], b_ref[...])
    if with_ple:
        gate = jax.nn.sigmoid(_dot(y.astype(BF16), wpg_ref[...]))
        y = y + gate * _dot(p_ref[...].astype(BF16), wpp_ref[...])
    o_ref[...] = y


def _ffn(x, weights, mix=None, ple=None):
    m, d = x.shape
    tm = min(ROW_TILE, m)
    row = lambda w: pl.BlockSpec((tm, w), lambda i: (i, 0))
    args, in_specs = [x], [row(d)]

    def add(picks):
        for array, spec in picks:
            args.append(array)
            in_specs.append(spec)

    if mix is not None:
        o_a, o_b = mix[:2]
        add([(o_a, row(o_a.shape[1])), (o_b, row(o_b.shape[1]))])
        add(mix[2:])
    add(weights)
    if ple is not None:
        (p, layer), wpg, wpp = ple
        add([(p, pl.BlockSpec((None, tm, p.shape[2]), lambda i: (layer, i, 0))), wpg, wpp])
    return pl.pallas_call(
        functools.partial(_ffn_body, with_mix=mix is not None, with_ple=ple is not None),
        grid=(m // tm,),
        in_specs=in_specs,
        out_specs=row(d),
        out_shape=jax.ShapeDtypeStruct((m, d), F32),
        compiler_params=_params("parallel"),
        name="ffn" + ("_mix" if mix is not None else "") + ("_ple" if ple is not None else ""),
    )(*args)


def _proj_body(x_ref, w_ref, wt_ref, qa_ref, ka_ref, qb_ref, kb_ref, vta_ref, vtb_ref, f_ref):
    xb = x_ref[0].astype(BF16)
    scale = HEAD_DIM ** -0.5 * LOG2_E

    def seg(n):
        return _dot(xb, w_ref[:, n * W_HEADS:(n + 1) * W_HEADS])

    qa_ref[0] = (seg(0) * scale).astype(BF16)
    ka_ref[0] = seg(1).astype(BF16)
    qb_ref[0] = (seg(2) * scale).astype(BF16)
    kb_ref[0] = seg(3).astype(BF16)
    rt = _dot_nt(wt_ref[...], xb)
    for n in range(vta_ref.shape[1]):
        cols = slice(n * KV_CHUNK, (n + 1) * KV_CHUNK)
        vta_ref[0, n] = rt[:W_HEADS, cols].astype(BF16)
        vtb_ref[0, n] = rt[W_HEADS:2 * W_HEADS, cols].astype(BF16)
    f_ref[0] = rt[2 * W_HEADS:, :]


def _attn_proj(x, w_rows, w_t):
    bsz, s, d = x.shape
    tm = min(ROW_TILE, s)
    n_chunk = tm // KV_CHUNK
    rows = pl.BlockSpec((1, tm, W_HEADS), lambda b, i: (b, i, 0))
    vts = pl.BlockSpec((1, n_chunk, W_HEADS, KV_CHUNK), lambda b, i: (b, i, 0, 0))
    row_t = jax.ShapeDtypeStruct((bsz, s, W_HEADS), BF16)
    vt_t = jax.ShapeDtypeStruct((bsz, s // KV_CHUNK, W_HEADS, KV_CHUNK), BF16)
    return pl.pallas_call(
        _proj_body,
        grid=(bsz, s // tm),
        in_specs=[pl.BlockSpec((1, tm, d), lambda b, i: (b, i, 0)),
                  _resident(w_rows.shape), _resident(w_t.shape)],
        out_specs=[rows, rows, rows, rows, vts, vts,
                   pl.BlockSpec((1, N_HEADS_B, tm), lambda b, i: (b, 0, i))],
        out_shape=[row_t, row_t, row_t, row_t, vt_t, vt_t,
                   jax.ShapeDtypeStruct((bsz, N_HEADS_B, s), F32)],
        compiler_params=_params("parallel", "parallel"),
        name="attn_proj",
    )(x, w_rows, w_t)


def _decay_body(f_ref, bf_ref, c_ref):
    x = f_ref[0] + bf_ref[...]
    log_f = jnp.minimum(x, 0.0) - jnp.log(1.0 + jnp.exp(-jnp.abs(x)))
    r = lax.broadcasted_iota(jnp.int32, (LANES, LANES), 0)
    c = lax.broadcasted_iota(jnp.int32, (LANES, LANES), 1)
    prefix = (r <= c).astype(BF16)
    carry = jnp.zeros((x.shape[0], 1), F32)
    for n in range(x.shape[1] // LANES):
        sl = slice(n * LANES, (n + 1) * LANES)
        rest = log_f[:, sl]
        within = jnp.zeros(rest.shape, F32)
        for _ in range(DECAY_PARTS):
            part = rest.astype(BF16)
            rest = rest - part.astype(F32)
            within = within + _dot(part, prefix)
        cs = within + carry
        carry = cs[:, LANES - 1:LANES]
        rest = cs * LOG2_E
        for part in range(DECAY_PARTS):
            term = rest.astype(BF16).astype(F32)
            c_ref[0, part, :, sl] = term
            rest = rest - term


def _fox_decay(f_t, b_f):
    bsz, h, s = f_t.shape
    return pl.pallas_call(
        _decay_body,
        grid=(bsz,),
        in_specs=[pl.BlockSpec((1, h, s), lambda b: (b, 0, 0)), _resident((h, 1))],
        out_specs=pl.BlockSpec((1, DECAY_PARTS, h, s), lambda b: (b, 0, 0, 0)),
        out_shape=jax.ShapeDtypeStruct((bsz, DECAY_PARTS, h, s), F32),
        compiler_params=_params("parallel"),
        name="fox_decay",
    )(f_t, b_f.reshape(h, 1))


def _bias_body(tbl_ref, o_ref):
    h = pl.program_id(0)
    r = lax.broadcasted_iota(jnp.int32, (MOBA_BLOCK, MOBA_BLOCK), 0)
    t = lax.broadcasted_iota(jnp.int32, (MOBA_BLOCK, MOBA_BLOCK), 1)
    max_exact = REL_BUCKETS // 2
    for slot in range(N_BIAS_SLOTS):
        n = jnp.maximum(t - r + slot * MOBA_BLOCK, 0)
        nf = jnp.maximum(n, max_exact).astype(F32)
        large = max_exact + (jnp.log(nf / max_exact) / math.log(REL_MAX_DIST / max_exact)
                             * (REL_BUCKETS - max_exact)).astype(jnp.int32)
        large = jnp.minimum(large, REL_BUCKETS - 1)
        bucket = jnp.where(n < max_exact, n, large)
        bias = jnp.zeros((MOBA_BLOCK, MOBA_BLOCK), F32)
        for k in range(REL_BUCKETS):
            bias = jnp.where(bucket == k, tbl_ref[k, h], bias)
        bias = bias * LOG2_E
        if slot == 0:
            bias = jnp.where(t >= r, bias, NEG_INF)
        o_ref[0, slot] = bias


def _moba_bias(rel_bias):
    shape = (N_HEADS_A, N_BIAS_SLOTS, MOBA_BLOCK, MOBA_BLOCK)
    return pl.pallas_call(
        _bias_body,
        grid=(N_HEADS_A,),
        in_specs=[pl.BlockSpec(memory_space=pltpu.SMEM)],
        out_specs=pl.BlockSpec((1,) + shape[1:], lambda h: (h, 0, 0, 0)),
        out_shape=jax.ShapeDtypeStruct(shape, F32),
        compiler_params=_params("parallel"),
        name="moba_bias",
    )(rel_bias)


def _lane_tile(x, head):
    tile = head // HEADS_PER_TILE
    return x[:, tile * LANES:(tile + 1) * LANES]


def _head_only(x, head):
    xt = _lane_tile(x, head)
    lane = lax.broadcasted_iota(jnp.int32, xt.shape, 1)
    return jnp.where(lane // HEAD_DIM == head % HEADS_PER_TILE, xt, jnp.zeros_like(xt))


def _chunk_rows(c):
    return pl.ds(pl.multiple_of(c * KV_CHUNK, KV_CHUNK), KV_CHUNK)


STREAMS = [(h, tile) for h in range(HEADS_PER_STEP) for tile in range(BLOCKS_PER_CHUNK)]


def _diag_keys(tile):
    return (tile + 1) * ATTN_TILE


def _flash_scratch():
    n = len(STREAMS)
    score_buf = pltpu.VMEM((n, KV_CHUNK, ATTN_TILE), F32)
    max_buf = pltpu.VMEM((n, 1, ATTN_TILE), F32)
    return [score_buf] * 3 + [max_buf] * 3 + [
        pltpu.VMEM((n, 1, ATTN_TILE), F32),
        pltpu.VMEM((n, HEAD_DIM + ONES_ROWS, ATTN_TILE), F32)]


def _next_step_spec(n_steps):
    return pl.BlockSpec((1, KV_CHUNK, STEP_LANES),
                        lambda b, p, g: (b, jnp.minimum(g + 1, n_steps - 1), p))


def _flash_step(g, n_steps, prepare, scorer, q_ref, qnext_ref, vt_ref, o_ref,
                s0_ref, s1_ref, sd_ref, mx0_ref, mx1_ref, mxd_ref, m_ref, acc_ref):
    diag_buf = 2
    bufs = ((s0_ref, mx0_ref), (s1_ref, mx1_ref), (sd_ref, mxd_ref))
    slot = g % 2
    ones = jnp.ones((ONES_ROWS, KV_CHUNK), BF16)

    def issue(score_fn, c, kind, buf):
        s_ref, mx_ref = bufs[buf]
        for n, s in enumerate(score_fn(c, kind)):
            s_ref[n, :s.shape[0]] = s
            mx_ref[n] = jnp.max(s, axis=0, keepdims=True)

    def absorb(c, buf, diag=False):
        s_ref, mx_ref = bufs[buf]
        for n, (h, tile) in enumerate(STREAMS):
            keys = _diag_keys(tile) if diag else KV_CHUNK
            m = m_ref[n]
            m_new = jnp.maximum(m, mx_ref[n])
            p = jnp.exp2(s_ref[n, :keys] - m_new).astype(BF16)
            vt = jnp.concatenate([vt_ref[0, c, h * HEAD_DIM:(h + 1) * HEAD_DIM, :keys],
                                  ones[:, :keys]], axis=0)
            acc_ref[n] = jnp.exp2(m - m_new) * acc_ref[n] + _dot(vt, p)
            m_ref[n] = m_new

    def start_step(step, q_rows_ref, step_slot):
        prepare(step, q_rows_ref, step_slot)
        issue(scorer(step, q_rows_ref, step_slot), step, "diag", diag_buf)

    def start_next():
        start_step(jnp.minimum(g + 1, n_steps - 1), qnext_ref, 1 - slot)

    @pl.when(g == 0)
    def _():
        start_step(g, q_ref, slot)

    m_ref[...] = jnp.full(m_ref.shape, NEG_INF, F32)
    acc_ref[...] = jnp.zeros(acc_ref.shape, F32)
    score_fn = scorer(g, q_ref, slot)

    @pl.when(g == 0)
    def _():
        absorb(g, diag_buf, diag=True)
        start_next()

    @pl.when(g > 0)
    def _():
        issue(score_fn, g - 1, "near", 1)
        absorb(g, diag_buf, diag=True)
        n_far = g - 1

        def pair(n, carry):
            c = g - 2 - 2 * n
            issue(score_fn, c, "far", 0)
            absorb(c + 1, 1)
            issue(score_fn, c - 1, "far", 1)
            absorb(c, 0)
            return carry

        lax.fori_loop(0, n_far // 2, pair, 0)

        @pl.when(n_far % 2 == 1)
        def _():
            issue(score_fn, 0, "far", 0)
            absorb(1, 1)
            start_next()
            absorb(0, 0)

        @pl.when(n_far % 2 == 0)
        def _():
            start_next()
            absorb(0, 1)

    for tile in range(BLOCKS_PER_CHUNK):
        outs = [acc_ref[n, :HEAD_DIM] / acc_ref[n, HEAD_DIM:HEAD_DIM + 1]
                for n, (_, stream_tile) in enumerate(STREAMS) if stream_tile == tile]
        o_ref[0, tile * ATTN_TILE:(tile + 1) * ATTN_TILE, :] = (
            jnp.concatenate(outs, axis=0).T.astype(o_ref.dtype))


def _fox_body(q_ref, qnext_ref, k_ref, cp_ref, vt_ref, o_ref, mask_ref, *flash_refs):
    g = pl.program_id(2)

    @pl.when(g == 0)
    def _():
        r = lax.broadcasted_iota(jnp.int32, (KV_CHUNK, ATTN_TILE), 0)
        t = lax.broadcasted_iota(jnp.int32, (KV_CHUNK, ATTN_TILE), 1)
        for tile in range(BLOCKS_PER_CHUNK):
            mask_ref[tile] = jnp.where(r <= t + tile * ATTN_TILE, 0.0, NEG_INF)

    def scorer(step, q_rows_ref, slot):
        lane = lax.broadcasted_iota(jnp.int32, (ATTN_TILE, LANES), 1)
        qxs = []
        for h, tile in STREAMS:
            q = q_rows_ref[0, tile * ATTN_TILE:(tile + 1) * ATTN_TILE, :]
            on_terms = (lane >= h * DECAY_PARTS) & (lane < (h + 1) * DECAY_PARTS)
            minus_one = jnp.where(on_terms, -1.0, 0.0).astype(BF16)
            qxs.append(jnp.concatenate([_head_only(q, h), minus_one], axis=1))

        def score_fn(c, kind):
            rows = _chunk_rows(c)
            k = k_ref[0, rows, :]
            terms = cp_ref[0, 0, rows, :]
            scores = []
            for (h, tile), qx in zip(STREAMS, qxs):
                kx = jnp.concatenate([_lane_tile(k, h), terms], axis=1)
                if kind == "diag":
                    keys = _diag_keys(tile)
                    scores.append(_dot_nt(kx[:keys], qx) + mask_ref[tile, :keys])
                else:
                    scores.append(_dot_nt(kx, qx))
            return scores

        return score_fn

    _flash_step(g, pl.num_programs(2), lambda step, q_rows_ref, slot: None, scorer, q_ref,
                qnext_ref, vt_ref, o_ref, *flash_refs)


def _moba_body(q_ref, qnext_ref, k_ref, vt_ref, bias_ref, o_ref, kbar_ref, kterms_ref, sel_ref,
               far_ref, *flash_refs):
    g = pl.program_id(2)
    n_blk = kbar_ref.shape[0]

    @pl.when(g == 0)
    def _():
        for n in range(n_blk):
            kb = k_ref[0, n * MOBA_BLOCK:(n + 1) * MOBA_BLOCK, :].astype(F32)
            kbar_ref[n:n + 1, :] = jnp.mean(kb, axis=0, keepdims=True)
        rest = kbar_ref[...]
        lane = lax.broadcasted_iota(jnp.int32, (n_blk, LANES), 1)
        for term in range(GATE_TERMS):
            part = rest.astype(BF16)
            rest = rest - part.astype(F32)
            for tile in range(STEP_LANES // LANES):
                part_tile = part[:, tile * LANES:(tile + 1) * LANES]
                for h in range(HEADS_PER_TILE):
                    row = (h * GATE_TERMS + term) * n_blk
                    kterms_ref[tile, row:row + n_blk, :] = jnp.where(
                        lane // HEAD_DIM == h, part_tile, jnp.zeros_like(part_tile))

    def q_tile(q_rows_ref, tile):
        return q_rows_ref[0, tile * ATTN_TILE:(tile + 1) * ATTN_TILE, :]

    def prepare(step, q_rows_ref, slot):
        blk = lax.broadcasted_iota(jnp.int32, (n_blk, ATTN_TILE), 0)
        tile_gates = {}
        for n, (h, tile) in enumerate(STREAMS):
            t = step * BLOCKS_PER_CHUNK + tile
            lanes = h // HEADS_PER_TILE
            if (lanes, tile) not in tile_gates:
                tile_gates[lanes, tile] = _dot_nt(kterms_ref[lanes],
                                                  _lane_tile(q_tile(q_rows_ref, tile), h))
            row = (h % HEADS_PER_TILE) * GATE_TERMS * n_blk
            gates = tile_gates[lanes, tile]
            gate = gates[row:row + n_blk]
            for term in range(1, GATE_TERMS):
                gate = gate + gates[row + term * n_blk:row + (term + 1) * n_blk]
            gate = jnp.where(blk < t, gate, NEG_INF)
            keep = blk < 0
            for _ in range(MOBA_TOPK):
                top = jnp.max(gate, axis=0, keepdims=True)
                first = jnp.min(jnp.where(gate == top, blk, n_blk), axis=0, keepdims=True)
                hit = blk == first
                keep = keep | (hit & (blk < t))
                gate = jnp.where(hit, -jnp.inf, gate)
            sel = jnp.where(keep, 0.0, NEG_INF)
            sel_ref[slot, n] = sel
            far_ref[slot, n] = sel + bias_ref[h, N_BIAS_SLOTS - 1, 0:1, 0:1]

    def scorer(step, q_rows_ref, slot):
        qms = [_head_only(q_tile(q_rows_ref, tile), h) for h, tile in STREAMS]

        def score_fn(c, kind):
            k = k_ref[0, _chunk_rows(c), :]
            chunks_back = {"diag": 0, "near": 1, "far": 2}[kind]
            out = []
            for n, ((h, tile), qm) in enumerate(zip(STREAMS, qms)):
                keys = _diag_keys(tile) if kind == "diag" else KV_CHUNK
                s = _dot_nt(_lane_tile(k, h)[:keys], qm)
                parts = []
                for b in range(keys // ATTN_TILE):
                    j = c * BLOCKS_PER_CHUNK + b
                    scores = s[b * ATTN_TILE:(b + 1) * ATTN_TILE]
                    back = chunks_back * BLOCKS_PER_CHUNK + tile - b
                    if back == 0:
                        parts.append(scores + bias_ref[h, 0])
                    elif back == 1:
                        parts.append(scores + bias_ref[h, 1] + sel_ref[slot, n, pl.ds(j, 1), :])
                    else:
                        parts.append(scores + far_ref[slot, n, pl.ds(j, 1), :])
                out.append(jnp.concatenate(parts, axis=0))
            return out

        return score_fn

    _flash_step(g, pl.num_programs(2), prepare, scorer, q_ref, qnext_ref, vt_ref, o_ref,
                *flash_refs)


def _attn_specs(bsz, s):
    n_blk = s // KV_CHUNK
    q_spec = pl.BlockSpec((1, KV_CHUNK, STEP_LANES), lambda b, p, i: (b, i, p))
    once = pl.Buffered(1)
    k_spec = pl.BlockSpec((1, s, STEP_LANES), lambda b, p, i: (b, 0, p), pipeline_mode=once)
    vt_spec = pl.BlockSpec((1, s // KV_CHUNK, STEP_LANES, KV_CHUNK), lambda b, p, i: (b, 0, p, 0),
                           pipeline_mode=once)
    out_shape = jax.ShapeDtypeStruct((bsz, s, W_HEADS), BF16)
    grid = (bsz, W_HEADS // STEP_LANES, n_blk)
    return grid, q_spec, k_spec, vt_spec, out_shape


def _fox_attention(q, k, decay_terms, vt):
    bsz, s, _ = q.shape
    grid, q_spec, k_spec, vt_spec, out_shape = _attn_specs(bsz, s)
    return pl.pallas_call(
        _fox_body,
        grid=grid,
        in_specs=[q_spec, _next_step_spec(grid[2]), k_spec,
                  pl.BlockSpec((1, 1, s, LANES), lambda b, p, i: (b, p, 0, 0),
                               pipeline_mode=pl.Buffered(1)),
                  vt_spec],
        out_specs=q_spec,
        out_shape=out_shape,
        scratch_shapes=[pltpu.VMEM((BLOCKS_PER_CHUNK, KV_CHUNK, ATTN_TILE), F32)]
        + _flash_scratch(),
        compiler_params=_params("parallel", "parallel", "arbitrary"),
        name="fox_attention",
    )(q, q, k, decay_terms, vt)


def _moba_attention(q, k, vt, bias):
    bsz, s, _ = q.shape
    grid, q_spec, k_spec, vt_spec, out_shape = _attn_specs(bsz, s)
    n_blk = s // MOBA_BLOCK
    return pl.pallas_call(
        _moba_body,
        grid=grid,
        in_specs=[q_spec, _next_step_spec(grid[2]), k_spec, vt_spec,
                  pl.BlockSpec((HEADS_PER_STEP, N_BIAS_SLOTS, MOBA_BLOCK, MOBA_BLOCK),
                               lambda b, p, i: (p, 0, 0, 0), pipeline_mode=pl.Buffered(1))],
        out_specs=q_spec,
        out_shape=out_shape,
        scratch_shapes=[pltpu.VMEM((n_blk, STEP_LANES), F32),
                        pltpu.VMEM((STEP_LANES // LANES, HEADS_PER_TILE * GATE_TERMS * n_blk, LANES),
                                   BF16),
                        pltpu.VMEM((2, len(STREAMS), n_blk, ATTN_TILE), F32),
                        pltpu.VMEM((2, len(STREAMS), n_blk, ATTN_TILE), F32)] + _flash_scratch(),
        compiler_params=_params("parallel", "parallel", "arbitrary"),
        name="moba_attention",
    )(q, q, k, vt, bias)


def _sgu_body(x_ref, win_ref, bin_ref, lg_ref, lb_ref, ws_ref, bs_ref, wout_ref, g_ref, b_ref,
              o_ref, y_ref):
    x = x_ref[...]
    xb = x.astype(BF16)
    w = x.shape[1]
    u = jax.nn.gelu(_dot(xb, win_ref[:, :w]) + bin_ref[:, :w])
    v = jax.nn.gelu(_dot(xb, win_ref[:, w:]) + bin_ref[:, w:])
    vb = _layer_norm(v, lg_ref[...], lb_ref[...]).astype(BF16)
    t = lax.broadcasted_iota(jnp.int32, (SGU_CHUNK, SGU_CHUNK), 0)
    s = lax.broadcasted_iota(jnp.int32, (SGU_CHUNK, SGU_CHUNK), 1)
    gw = w // SGU_GROUPS
    for g in range(SGU_GROUPS):
        w_g = jnp.where(t >= s, ws_ref[g], 0.0).astype(BF16)
        cols = slice(g * gw, (g + 1) * gw)
        for c in range(x.shape[0] // SGU_CHUNK):
            rows = slice(c * SGU_CHUNK, (c + 1) * SGU_CHUNK)
            mixed = _dot(w_g, vb[rows, cols]) + bs_ref[:, g:g + 1]
            y_ref[rows, cols] = (u[rows, cols] * mixed).astype(BF16)
    out = _dot(y_ref[...], wout_ref[...])
    o_ref[...] = _layer_norm(DN_ALPHA * x + out, g_ref[...], b_ref[...])


def _sgu(x, w_in, b_in, ln_g, ln_b, w_s, b_s_t, w_out, g, b):
    m, d = x.shape
    tm = min(ROW_TILE, m)
    row = pl.BlockSpec((tm, d), lambda i: (i, 0))
    weights = [w_in, b_in, ln_g, ln_b, w_s, b_s_t, w_out, g, b]
    return pl.pallas_call(
        _sgu_body,
        grid=(m // tm,),
        in_specs=[row] + [_resident(a.shape) for a in weights],
        out_specs=row,
        out_shape=jax.ShapeDtypeStruct((m, d), F32),
        scratch_shapes=[pltpu.VMEM((tm, d), BF16)],
        compiler_params=_params("parallel"),
        name="sgu",
    )(x, *weights)


def _mixer_attention(x2, bsz, s, w_in, b_f, rel_bias):
    d = x2.shape[1]
    seg = lambda n: w_in[:, n * W_HEADS:(n + 1) * W_HEADS]
    w_rows = jnp.concatenate([seg(0), seg(1), seg(3), seg(4)], axis=1).astype(BF16)
    w_t = jnp.concatenate([seg(2), seg(5), w_in[:, 6 * W_HEADS:]], axis=1).T.astype(BF16)
    qa, ka, qb, kb, vta, vtb, f_t = _attn_proj(x2.reshape(bsz, s, d), w_rows, w_t)
    n_groups = N_HEADS_B // HEADS_PER_STEP
    terms = _fox_decay(f_t, b_f).reshape(bsz, DECAY_PARTS, n_groups, HEADS_PER_STEP, s)
    terms = terms.transpose(0, 2, 4, 3, 1).reshape(bsz, n_groups, s, HEADS_PER_STEP * DECAY_PARTS)
    terms = jnp.pad(terms.astype(BF16), ((0, 0),) * 3 + ((0, LANES - terms.shape[-1]),))
    o_a = _moba_attention(qa, ka, vta, _moba_bias(rel_bias))
    o_b = _fox_attention(qb, kb, terms, vtb)
    return o_a.reshape(bsz * s, W_HEADS), o_b.reshape(bsz * s, W_HEADS)


def kernel(x, p, ln_g, ln_b, ffn_w_gate, ffn_w_up, ffn_w_down, attn_w_in, attn_b_f, attn_w_out,
           rel_bias, sgu_w_in, sgu_b_in, sgu_ln_g, sgu_ln_b, sgu_w_s, sgu_b_s, sgu_w_out,
           ple_w_proj, ple_w_gate):
    bsz, s, d = x.shape
    x2 = x.reshape(bsz * s, d)
    vec = lambda a: a.reshape(1, -1)
    wg, wu, wd = ffn_w_gate.astype(BF16), ffn_w_up.astype(BF16), ffn_w_down.astype(BF16)
    ln_g4, ln_b4 = ln_g.reshape(DEPTH, 3, 1, d), ln_b.reshape(DEPTH, 3, 1, d)
    p3 = p.reshape(DEPTH, bsz * s, p.shape[-1])
    w_ple_gate, w_ple_proj = ple_w_gate.astype(BF16), ple_w_proj.astype(BF16)
    w_attn_out = attn_w_out.astype(BF16).reshape(attn_w_out.shape[0], 2, W_HEADS, d)
    for i in range(DEPTH):
        j = i // 2
        ln = lambda n: [_pick(ln_g4, i, n), _pick(ln_b4, i, n)]
        ffn_w = lambda n: [_pick(wg, i, n), _pick(wu, i, n), _pick(wd, i, n)] + ln(2 * n)
        x2 = _ffn(x2, ffn_w(0))
        ple = ((p3, i), _pick(w_ple_gate, i), _pick(w_ple_proj, i))
        if i % 2 == 0:
            o_a, o_b = _mixer_attention(x2, bsz, s, attn_w_in[j], attn_b_f[j], rel_bias)
            mix = [o_a, o_b, _pick(w_attn_out, j, 0), _pick(w_attn_out, j, 1)] + ln(1)
            x2 = _ffn(x2, ffn_w(1), mix=mix, ple=ple)
        else:
            x2 = _sgu(x2, sgu_w_in[j].astype(BF16), vec(sgu_b_in[j]), vec(sgu_ln_g[j]),
                      vec(sgu_ln_b[j]), sgu_w_s[j], sgu_b_s[j].T, sgu_w_out[j].astype(BF16),
                      vec(ln_g[i, 1]), vec(ln_b[i, 1]))
            x2 = _ffn(x2, ffn_w(1), ple=ple)
    return x2.reshape(bsz, s, d)
```

```python
import functools
import math

import jax
import jax.numpy as jnp
from jax import lax
from jax.experimental import pallas as pl
from jax.experimental.pallas import tpu as pltpu

D_MODEL = 1024
HEAD_DIM = 64
N_HEADS_A = 8
N_HEADS_B = 8
W_HEADS = N_HEADS_A * HEAD_DIM
MOBA_BLOCK = 256
MOBA_TOPK = 3
REL_BUCKETS = 32
REL_MAX_DIST = 128
SGU_CHUNK = 128
SGU_GROUPS = 8
D_FF = 2816
PLE_DIM = 256
DEPTH = 2
DN_ALPHA = (2.0 * DEPTH) ** 0.25
LN_EPS = 1e-5
NEG_INF = -1e30
LOG2_E = math.log2(math.e)

F32 = jnp.float32
BF16 = jnp.bfloat16

VMEM_LIMIT_BYTES = 56 * 1024 * 1024
LANES = 128
HEADS_PER_TILE = LANES // HEAD_DIM
HEADS_PER_STEP = 4
STEP_LANES = HEADS_PER_STEP * HEAD_DIM
FF_CHUNK = 256
ROW_TILE = 512
ATTN_TILE = MOBA_BLOCK
KV_CHUNK = 512
BLOCKS_PER_CHUNK = KV_CHUNK // ATTN_TILE
N_BIAS_SLOTS = 3
DECAY_PARTS = 3
GATE_TERMS = 3
ONES_ROWS = 16


def _params(*semantics):
    return pltpu.CompilerParams(dimension_semantics=semantics, vmem_limit_bytes=VMEM_LIMIT_BYTES)


def _resident(shape):
    zeros = (0,) * len(shape)
    return pl.BlockSpec(shape, lambda *_: zeros, pipeline_mode=pl.Buffered(1))


def _pick(array, *lead):
    tail = array.shape[len(lead):]
    index = tuple(lead) + (0,) * len(tail)
    spec = pl.BlockSpec((None,) * len(lead) + tail, lambda *_: index, pipeline_mode=pl.Buffered(1))
    return array, spec


def _layer_norm(y, g, b):
    mu = jnp.mean(y, axis=-1, keepdims=True)
    d = y - mu
    var = jnp.mean(d * d, axis=-1, keepdims=True)
    return d * lax.rsqrt(var + LN_EPS) * g + b


def _dot(a, b):
    return jnp.dot(a, b, preferred_element_type=F32)


def _dot_nt(a, b):
    return lax.dot_general(a, b, (((1,), (1,)), ((), ())), preferred_element_type=F32)


def _ffn_body(x_ref, *refs, with_mix, with_ple):
    refs = list(refs)
    o_ref = refs.pop()
    x = x_ref[...]
    if with_mix:
        oa_ref, ob_ref, wa_ref, wb_ref, mg_ref, mb_ref = refs[:6]
        refs = refs[6:]
        mixed = _dot(oa_ref[...], wa_ref[...]) + _dot(ob_ref[...], wb_ref[...])
        x = _layer_norm(DN_ALPHA * x + mixed, mg_ref[...], mb_ref[...])
    wg_ref, wu_ref, wd_ref, g_ref, b_ref = refs[:5]
    if with_ple:
        p_ref, wpg_ref, wpp_ref = refs[5:]
    xb = x.astype(BF16)
    acc = jnp.zeros(x.shape, F32)
    for c in range(D_FF // FF_CHUNK):
        sl = slice(c * FF_CHUNK, (c + 1) * FF_CHUNK)
        gate = _dot(xb, wg_ref[:, sl])
        up = _dot(xb, wu_ref[:, sl])
        h = (jax.nn.silu(gate) * up).astype(BF16)
        acc = acc + _dot(h, wd_ref[sl, :])
    y = _layer_norm(DN_ALPHA * x + 0.5 * acc, g_ref[...], b_ref[...])
    if with_ple:
        gate = jax.nn.sigmoid(_dot(y.astype(BF16), wpg_ref[...]))
        y = y + gate * _dot(p_ref[...].astype(BF16), wpp_ref[...])
    o_ref[...] = y


def _ffn(x, weights, mix=None, ple=None):
    m, d = x.shape
    tm = min(ROW_TILE, m)
    row = lambda w: pl.BlockSpec((tm, w), lambda i: (i, 0))
    args, in_specs = [x], [row(d)]

    def add(picks):
        for array, spec in picks:
            args.append(array)
            in_specs.append(spec)

    if mix is not None:
        o_a, o_b = mix[:2]
        add([(o_a, row(o_a.shape[1])), (o_b, row(o_b.shape[1]))])
        add(mix[2:])
    add(weights)
    if ple is not None:
        (p, layer), wpg, wpp = ple
        add([(p, pl.BlockSpec((None, tm, p.shape[2]), lambda i: (layer, i, 0))), wpg, wpp])
    return pl.pallas_call(
        functools.partial(_ffn_body, with_mix=mix is not None, with_ple=ple is not None),
        grid=(m // tm,),
        in_specs=in_specs,
        out_specs=row(d),
        out_shape=jax.ShapeDtypeStruct((m, d), F32),
        compiler_params=_params("parallel"),
        name="ffn" + ("_mix" if mix is not None else "") + ("_ple" if ple is not None else ""),
    )(*args)


def _proj_body(x_ref, w_ref, wt_ref, qa_ref, ka_ref, qb_ref, kb_ref, vta_ref, vtb_ref, f_ref):
    xb = x_ref[0].astype(BF16)
    scale = HEAD_DIM ** -0.5 * LOG2_E

    def seg(n):
        return _dot(xb, w_ref[:, n * W_HEADS:(n + 1) * W_HEADS])

    qa_ref[0] = (seg(0) * scale).astype(BF16)
    ka_ref[0] = seg(1).astype(BF16)
    qb_ref[0] = (seg(2) * scale).astype(BF16)
    kb_ref[0] = seg(3).astype(BF16)
    rt = _dot_nt(wt_ref[...], xb)
    for n in range(vta_ref.shape[1]):
        cols = slice(n * KV_CHUNK, (n + 1) * KV_CHUNK)
        vta_ref[0, n] = rt[:W_HEADS, cols].astype(BF16)
        vtb_ref[0, n] = rt[W_HEADS:2 * W_HEADS, cols].astype(BF16)
    f_ref[0] = rt[2 * W_HEADS:, :]


def _attn_proj(x, w_rows, w_t):
    bsz, s, d = x.shape
    tm = min(ROW_TILE, s)
    n_chunk = tm // KV_CHUNK
    rows = pl.BlockSpec((1, tm, W_HEADS), lambda b, i: (b, i, 0))
    vts = pl.BlockSpec((1, n_chunk, W_HEADS, KV_CHUNK), lambda b, i: (b, i, 0, 0))
    row_t = jax.ShapeDtypeStruct((bsz, s, W_HEADS), BF16)
    vt_t = jax.ShapeDtypeStruct((bsz, s // KV_CHUNK, W_HEADS, KV_CHUNK), BF16)
    return pl.pallas_call(
        _proj_body,
        grid=(bsz, s // tm),
        in_specs=[pl.BlockSpec((1, tm, d), lambda b, i: (b, i, 0)),
                  _resident(w_rows.shape), _resident(w_t.shape)],
        out_specs=[rows, rows, rows, rows, vts, vts,
                   pl.BlockSpec((1, N_HEADS_B, tm), lambda b, i: (b, 0, i))],
        out_shape=[row_t, row_t, row_t, row_t, vt_t, vt_t,
                   jax.ShapeDtypeStruct((bsz, N_HEADS_B, s), F32)],
        compiler_params=_params("parallel", "parallel"),
        name="attn_proj",
    )(x, w_rows, w_t)


def _decay_body(f_ref, bf_ref, c_ref):
    x = f_ref[0] + bf_ref[...]
    log_f = jnp.minimum(x, 0.0) - jnp.log(1.0 + jnp.exp(-jnp.abs(x)))
    r = lax.broadcasted_iota(jnp.int32, (LANES, LANES), 0)
    c = lax.broadcasted_iota(jnp.int32, (LANES, LANES), 1)
    prefix = (r <= c).astype(BF16)
    carry = jnp.zeros((x.shape[0], 1), F32)
    for n in range(x.shape[1] // LANES):
        sl = slice(n * LANES, (n + 1) * LANES)
        rest = log_f[:, sl]
        within = jnp.zeros(rest.shape, F32)
        for _ in range(DECAY_PARTS):
            part = rest.astype(BF16)
            rest = rest - part.astype(F32)
            within = within + _dot(part, prefix)
        cs = within + carry
        carry = cs[:, LANES - 1:LANES]
        rest = cs * LOG2_E
        for part in range(DECAY_PARTS):
            term = rest.astype(BF16).astype(F32)
            c_ref[0, part, :, sl] = term
            rest = rest - term


def _fox_decay(f_t, b_f):
    bsz, h, s = f_t.shape
    return pl.pallas_call(
        _decay_body,
        grid=(bsz,),
        in_specs=[pl.BlockSpec((1, h, s), lambda b: (b, 0, 0)), _resident((h, 1))],
        out_specs=pl.BlockSpec((1, DECAY_PARTS, h, s), lambda b: (b, 0, 0, 0)),
        out_shape=jax.ShapeDtypeStruct((bsz, DECAY_PARTS, h, s), F32),
        compiler_params=_params("parallel"),
        name="fox_decay",
    )(f_t, b_f.reshape(h, 1))


def _bias_body(tbl_ref, o_ref):
    h = pl.program_id(0)
    r = lax.broadcasted_iota(jnp.int32, (MOBA_BLOCK, MOBA_BLOCK), 0)
    t = lax.broadcasted_iota(jnp.int32, (MOBA_BLOCK, MOBA_BLOCK), 1)
    max_exact = REL_BUCKETS // 2
    for slot in range(N_BIAS_SLOTS):
        n = jnp.maximum(t - r + slot * MOBA_BLOCK, 0)
        nf = jnp.maximum(n, max_exact).astype(F32)
        large = max_exact + (jnp.log(nf / max_exact) / math.log(REL_MAX_DIST / max_exact)
                             * (REL_BUCKETS - max_exact)).astype(jnp.int32)
        large = jnp.minimum(large, REL_BUCKETS - 1)
        bucket = jnp.where(n < max_exact, n, large)
        bias = jnp.zeros((MOBA_BLOCK, MOBA_BLOCK), F32)
        for k in range(REL_BUCKETS):
            bias = jnp.where(bucket == k, tbl_ref[k, h], bias)
        bias = bias * LOG2_E
        if slot == 0:
            bias = jnp.where(t >= r, bias, NEG_INF)
        o_ref[0, slot] = bias


def _moba_bias(rel_bias):
    shape = (N_HEADS_A, N_BIAS_SLOTS, MOBA_BLOCK, MOBA_BLOCK)
    return pl.pallas_call(
        _bias_body,
        grid=(N_HEADS_A,),
        in_specs=[pl.BlockSpec(memory_space=pltpu.SMEM)],
        out_specs=pl.BlockSpec((1,) + shape[1:], lambda h: (h, 0, 0, 0)),
        out_shape=jax.ShapeDtypeStruct(shape, F32),
        compiler_params=_params("parallel"),
        name="moba_bias",
    )(rel_bias)


def _tile_lanes(head):
    tile = head // HEADS_PER_TILE
    return slice(tile * LANES, (tile + 1) * LANES)


def _lane_tile(x, head):
    return x[:, _tile_lanes(head)]


def _head_only(x, head):
    xt = _lane_tile(x, head)
    lane = lax.broadcasted_iota(jnp.int32, xt.shape, 1)
    return jnp.where(lane // HEAD_DIM == head % HEADS_PER_TILE, xt, jnp.zeros_like(xt))


def _chunk_rows(c):
    return pl.ds(pl.multiple_of(c * KV_CHUNK, KV_CHUNK), KV_CHUNK)


STREAMS = [(h, tile) for h in range(HEADS_PER_STEP) for tile in range(BLOCKS_PER_CHUNK)]


def _diag_keys(tile):
    return (tile + 1) * ATTN_TILE


def _flash_scratch():
    n = len(STREAMS)
    score_buf = pltpu.VMEM((n, KV_CHUNK, ATTN_TILE), F32)
    max_buf = pltpu.VMEM((n, 1, ATTN_TILE), F32)
    return [score_buf] * 3 + [max_buf] * 3 + [
        pltpu.VMEM((n, 1, ATTN_TILE), F32),
        pltpu.VMEM((n, HEAD_DIM + ONES_ROWS, ATTN_TILE), F32)]


def _next_step_spec(n_steps):
    return pl.BlockSpec((1, KV_CHUNK, STEP_LANES),
                        lambda b, p, g: (b, jnp.minimum(g + 1, n_steps - 1), p))


def _flash_step(g, n_steps, prepare, scorer, q_ref, qnext_ref, vt_ref, o_ref,
                s0_ref, s1_ref, sd_ref, mx0_ref, mx1_ref, mxd_ref, m_ref, acc_ref):
    diag_buf = 2
    bufs = ((s0_ref, mx0_ref), (s1_ref, mx1_ref), (sd_ref, mxd_ref))
    slot = g % 2
    ones = jnp.ones((ONES_ROWS, KV_CHUNK), BF16)

    def issue(score_fn, n, c, kind, buf):
        s_ref, mx_ref = bufs[buf]
        s = score_fn(n, c, kind)
        s_ref[n, :s.shape[0]] = s
        mx_ref[n] = jnp.max(s, axis=0, keepdims=True)

    def absorb(n, c, buf, diag=False):
        s_ref, mx_ref = bufs[buf]
        h, tile = STREAMS[n]
        keys = _diag_keys(tile) if diag else KV_CHUNK
        m = m_ref[n]
        m_new = jnp.maximum(m, mx_ref[n])
        p = jnp.exp2(s_ref[n, :keys] - m_new).astype(BF16)
        vt = jnp.concatenate([vt_ref[0, c, h * HEAD_DIM:(h + 1) * HEAD_DIM, :keys],
                              ones[:, :keys]], axis=0)
        acc_ref[n] = jnp.exp2(m - m_new) * acc_ref[n] + _dot(vt, p)
        m_ref[n] = m_new

    def overlap(score_fn, c_issue, kind, buf_issue, c_absorb, buf_absorb, diag=False):
        for n in range(len(STREAMS)):
            issue(score_fn, n, c_issue, kind, buf_issue)
            absorb(n, c_absorb, buf_absorb, diag)

    def start_step(step, q_rows_ref, step_slot, c_absorb=None, buf_absorb=None):
        prepare(step, q_rows_ref, step_slot)
        score_fn = scorer(step, q_rows_ref, step_slot)
        if c_absorb is None:
            for n in range(len(STREAMS)):
                issue(score_fn, n, step, "diag", diag_buf)
        else:
            overlap(score_fn, step, "diag", diag_buf, c_absorb, buf_absorb)

    def start_next(c_absorb=None, buf_absorb=None):
        start_step(jnp.minimum(g + 1, n_steps - 1), qnext_ref, 1 - slot, c_absorb, buf_absorb)

    @pl.when(g == 0)
    def _():
        start_step(g, q_ref, slot)

    m_ref[...] = jnp.full(m_ref.shape, NEG_INF, F32)
    acc_ref[...] = jnp.zeros(acc_ref.shape, F32)
    score_fn = scorer(g, q_ref, slot)

    @pl.when(g == 0)
    def _():
        for n in range(len(STREAMS)):
            absorb(n, g, diag_buf, diag=True)
        start_next()

    @pl.when(g > 0)
    def _():
        overlap(score_fn, g - 1, "near", 1, g, diag_buf, diag=True)
        n_far = g - 1

        def pair(n, carry):
            c = g - 2 - 2 * n
            overlap(score_fn, c, "far", 0, c + 1, 1)
            overlap(score_fn, c - 1, "far", 1, c, 0)
            return carry

        lax.fori_loop(0, n_far // 2, pair, 0)

        @pl.when(n_far % 2 == 1)
        def _():
            overlap(score_fn, 0, "far", 0, 1, 1)
            start_next(0, 0)

        @pl.when(n_far % 2 == 0)
        def _():
            start_next(0, 1)

    for tile in range(BLOCKS_PER_CHUNK):
        outs = [acc_ref[n, :HEAD_DIM] / acc_ref[n, HEAD_DIM:HEAD_DIM + 1]
                for n, (_, stream_tile) in enumerate(STREAMS) if stream_tile == tile]
        o_ref[0, tile * ATTN_TILE:(tile + 1) * ATTN_TILE, :] = (
            jnp.concatenate(outs, axis=0).T.astype(o_ref.dtype))


def _fox_body(q_ref, qnext_ref, k_ref, cp_ref, vt_ref, o_ref, mask_ref, *flash_refs):
    g = pl.program_id(2)

    @pl.when(g == 0)
    def _():
        r = lax.broadcasted_iota(jnp.int32, (KV_CHUNK, ATTN_TILE), 0)
        t = lax.broadcasted_iota(jnp.int32, (KV_CHUNK, ATTN_TILE), 1)
        for tile in range(BLOCKS_PER_CHUNK):
            mask_ref[tile] = jnp.where(r <= t + tile * ATTN_TILE, 0.0, NEG_INF)

    def scorer(step, q_rows_ref, slot):
        lane = lax.broadcasted_iota(jnp.int32, (ATTN_TILE, LANES), 1)
        qxs = []
        for h, tile in STREAMS:
            q = q_rows_ref[0, tile * ATTN_TILE:(tile + 1) * ATTN_TILE, :]
            on_terms = (lane >= h * DECAY_PARTS) & (lane < (h + 1) * DECAY_PARTS)
            minus_one = jnp.where(on_terms, -1.0, 0.0).astype(BF16)
            qxs.append(jnp.concatenate([_head_only(q, h), minus_one], axis=1))

        def score_fn(n, c, kind):
            h, tile = STREAMS[n]
            rows = _chunk_rows(c)
            kx = jnp.concatenate([k_ref[0, rows, _tile_lanes(h)], cp_ref[0, 0, rows, :]], axis=1)
            if kind == "diag":
                keys = _diag_keys(tile)
                return _dot_nt(kx[:keys], qxs[n]) + mask_ref[tile, :keys]
            return _dot_nt(kx, qxs[n])

        return score_fn

    _flash_step(g, pl.num_programs(2), lambda step, q_rows_ref, slot: None, scorer, q_ref,
                qnext_ref, vt_ref, o_ref, *flash_refs)


def _moba_body(q_ref, qnext_ref, k_ref, vt_ref, bias_ref, o_ref, kbar_ref, kterms_ref, sel_ref,
               far_ref, *flash_refs):
    g = pl.program_id(2)
    n_blk = kbar_ref.shape[0]

    @pl.when(g == 0)
    def _():
        for n in range(n_blk):
            kb = k_ref[0, n * MOBA_BLOCK:(n + 1) * MOBA_BLOCK, :].astype(F32)
            kbar_ref[n:n + 1, :] = jnp.mean(kb, axis=0, keepdims=True)
        rest = kbar_ref[...]
        lane = lax.broadcasted_iota(jnp.int32, (n_blk, LANES), 1)
        for term in range(GATE_TERMS):
            part = rest.astype(BF16)
            rest = rest - part.astype(F32)
            for tile in range(STEP_LANES // LANES):
                part_tile = part[:, tile * LANES:(tile + 1) * LANES]
                for h in range(HEADS_PER_TILE):
                    row = (h * GATE_TERMS + term) * n_blk
                    kterms_ref[tile, row:row + n_blk, :] = jnp.where(
                        lane // HEAD_DIM == h, part_tile, jnp.zeros_like(part_tile))

    def q_tile(q_rows_ref, tile):
        return q_rows_ref[0, tile * ATTN_TILE:(tile + 1) * ATTN_TILE, :]

    def prepare(step, q_rows_ref, slot):
        blk = lax.broadcasted_iota(jnp.int32, (n_blk, ATTN_TILE), 0)
        tile_gates = {}
        for n, (h, tile) in enumerate(STREAMS):
            t = step * BLOCKS_PER_CHUNK + tile
            lanes = h // HEADS_PER_TILE
            if (lanes, tile) not in tile_gates:
                tile_gates[lanes, tile] = _dot_nt(kterms_ref[lanes],
                                                  _lane_tile(q_tile(q_rows_ref, tile), h))
            row = (h % HEADS_PER_TILE) * GATE_TERMS * n_blk
            gates = tile_gates[lanes, tile]
            gate = gates[row:row + n_blk]
            for term in range(1, GATE_TERMS):
                gate = gate + gates[row + term * n_blk:row + (term + 1) * n_blk]
            gate = jnp.where(blk < t, gate, NEG_INF)
            keep = blk < 0
            for _ in range(MOBA_TOPK):
                top = jnp.max(gate, axis=0, keepdims=True)
                first = jnp.min(jnp.where(gate == top, blk, n_blk), axis=0, keepdims=True)
                hit = blk == first
                keep = keep | (hit & (blk < t))
                gate = jnp.where(hit, -jnp.inf, gate)
            sel = jnp.where(keep, 0.0, NEG_INF)
            sel_ref[slot, n] = sel
            far_ref[slot, n] = sel + bias_ref[h, N_BIAS_SLOTS - 1, 0:1, 0:1]

    def scorer(step, q_rows_ref, slot):
        qms = [_head_only(q_tile(q_rows_ref, tile), h) for h, tile in STREAMS]

        def score_fn(n, c, kind):
            h, tile = STREAMS[n]
            chunks_back = {"diag": 0, "near": 1, "far": 2}[kind]
            keys = _diag_keys(tile) if kind == "diag" else KV_CHUNK
            s = _dot_nt(k_ref[0, _chunk_rows(c), _tile_lanes(h)][:keys], qms[n])
            parts = []
            for b in range(keys // ATTN_TILE):
                j = c * BLOCKS_PER_CHUNK + b
                scores = s[b * ATTN_TILE:(b + 1) * ATTN_TILE]
                back = chunks_back * BLOCKS_PER_CHUNK + tile - b
                if back == 0:
                    parts.append(scores + bias_ref[h, 0])
                elif back == 1:
                    parts.append(scores + bias_ref[h, 1] + sel_ref[slot, n, pl.ds(j, 1), :])
                else:
                    parts.append(scores + far_ref[slot, n, pl.ds(j, 1), :])
            return jnp.concatenate(parts, axis=0)

        return score_fn

    _flash_step(g, pl.num_programs(2), prepare, scorer, q_ref, qnext_ref, vt_ref, o_ref,
                *flash_refs)


def _attn_specs(bsz, s):
    n_blk = s // KV_CHUNK
    q_spec = pl.BlockSpec((1, KV_CHUNK, STEP_LANES), lambda b, p, i: (b, i, p))
    once = pl.Buffered(1)
    k_spec = pl.BlockSpec((1, s, STEP_LANES), lambda b, p, i: (b, 0, p), pipeline_mode=once)
    vt_spec = pl.BlockSpec((1, s // KV_CHUNK, STEP_LANES, KV_CHUNK), lambda b, p, i: (b, 0, p, 0),
                           pipeline_mode=once)
    out_shape = jax.ShapeDtypeStruct((bsz, s, W_HEADS), BF16)
    grid = (bsz, W_HEADS // STEP_LANES, n_blk)
    return grid, q_spec, k_spec, vt_spec, out_shape


def _fox_attention(q, k, decay_terms, vt):
    bsz, s, _ = q.shape
    grid, q_spec, k_spec, vt_spec, out_shape = _attn_specs(bsz, s)
    return pl.pallas_call(
        _fox_body,
        grid=grid,
        in_specs=[q_spec, _next_step_spec(grid[2]), k_spec,
                  pl.BlockSpec((1, 1, s, LANES), lambda b, p, i: (b, p, 0, 0),
                               pipeline_mode=pl.Buffered(1)),
                  vt_spec],
        out_specs=q_spec,
        out_shape=out_shape,
        scratch_shapes=[pltpu.VMEM((BLOCKS_PER_CHUNK, KV_CHUNK, ATTN_TILE), F32)]
        + _flash_scratch(),
        compiler_params=_params("parallel", "parallel", "arbitrary"),
        name="fox_attention",
    )(q, q, k, decay_terms, vt)


def _moba_attention(q, k, vt, bias):
    bsz, s, _ = q.shape
    grid, q_spec, k_spec, vt_spec, out_shape = _attn_specs(bsz, s)
    n_blk = s // MOBA_BLOCK
    return pl.pallas_call(
        _moba_body,
        grid=grid,
        in_specs=[q_spec, _next_step_spec(grid[2]), k_spec, vt_spec,
                  pl.BlockSpec((HEADS_PER_STEP, N_BIAS_SLOTS, MOBA_BLOCK, MOBA_BLOCK),
                               lambda b, p, i: (p, 0, 0, 0), pipeline_mode=pl.Buffered(1))],
        out_specs=q_spec,
        out_shape=out_shape,
        scratch_shapes=[pltpu.VMEM((n_blk, STEP_LANES), F32),
                        pltpu.VMEM((STEP_LANES // LANES, HEADS_PER_TILE * GATE_TERMS * n_blk, LANES),
                                   BF16),
                        pltpu.VMEM((2, len(STREAMS), n_blk, ATTN_TILE), F32),
                        pltpu.VMEM((2, len(STREAMS), n_blk, ATTN_TILE), F32)] + _flash_scratch(),
        compiler_params=_params("parallel", "parallel", "arbitrary"),
        name="moba_attention",
    )(q, q, k, vt, bias)


def _sgu_body(x_ref, win_ref, bin_ref, lg_ref, lb_ref, ws_ref, bs_ref, wout_ref, g_ref, b_ref,
              o_ref, y_ref):
    x = x_ref[...]
    xb = x.astype(BF16)
    w = x.shape[1]
    u = jax.nn.gelu(_dot(xb, win_ref[:, :w]) + bin_ref[:, :w])
    v = jax.nn.gelu(_dot(xb, win_ref[:, w:]) + bin_ref[:, w:])
    vb = _layer_norm(v, lg_ref[...], lb_ref[...]).astype(BF16)
    t = lax.broadcasted_iota(jnp.int32, (SGU_CHUNK, SGU_CHUNK), 0)
    s = lax.broadcasted_iota(jnp.int32, (SGU_CHUNK, SGU_CHUNK), 1)
    gw = w // SGU_GROUPS
    for g in range(SGU_GROUPS):
        w_g = jnp.where(t >= s, ws_ref[g], 0.0).astype(BF16)
        cols = slice(g * gw, (g + 1) * gw)
        for c in range(x.shape[0] // SGU_CHUNK):
            rows = slice(c * SGU_CHUNK, (c + 1) * SGU_CHUNK)
            mixed = _dot(w_g, vb[rows, cols]) + bs_ref[:, g:g + 1]
            y_ref[rows, cols] = (u[rows, cols] * mixed).astype(BF16)
    out = _dot(y_ref[...], wout_ref[...])
    o_ref[...] = _layer_norm(DN_ALPHA * x + out, g_ref[...], b_ref[...])


def _sgu(x, w_in, b_in, ln_g, ln_b, w_s, b_s_t, w_out, g, b):
    m, d = x.shape
    tm = min(ROW_TILE, m)
    row = pl.BlockSpec((tm, d), lambda i: (i, 0))
    weights = [w_in, b_in, ln_g, ln_b, w_s, b_s_t, w_out, g, b]
    return pl.pallas_call(
        _sgu_body,
        grid=(m // tm,),
        in_specs=[row] + [_resident(a.shape) for a in weights],
        out_specs=row,
        out_shape=jax.ShapeDtypeStruct((m, d), F32),
        scratch_shapes=[pltpu.VMEM((tm, d), BF16)],
        compiler_params=_params("parallel"),
        name="sgu",
    )(x, *weights)


def _mixer_attention(x2, bsz, s, w_in, b_f, rel_bias):
    d = x2.shape[1]
    seg = lambda n: w_in[:, n * W_HEADS:(n + 1) * W_HEADS]
    w_rows = jnp.concatenate([seg(0), seg(1), seg(3), seg(4)], axis=1).astype(BF16)
    w_t = jnp.concatenate([seg(2), seg(5), w_in[:, 6 * W_HEADS:]], axis=1).T.astype(BF16)
    qa, ka, qb, kb, vta, vtb, f_t = _attn_proj(x2.reshape(bsz, s, d), w_rows, w_t)
    n_groups = N_HEADS_B // HEADS_PER_STEP
    terms = _fox_decay(f_t, b_f).reshape(bsz, DECAY_PARTS, n_groups, HEADS_PER_STEP, s)
    terms = terms.transpose(0, 2, 4, 3, 1).reshape(bsz, n_groups, s, HEADS_PER_STEP * DECAY_PARTS)
    terms = jnp.pad(terms.astype(BF16), ((0, 0),) * 3 + ((0, LANES - terms.shape[-1]),))
    o_a = _moba_attention(qa, ka, vta, _moba_bias(rel_bias))
    o_b = _fox_attention(qb, kb, terms, vtb)
    return o_a.reshape(bsz * s, W_HEADS), o_b.reshape(bsz * s, W_HEADS)


def kernel(x, p, ln_g, ln_b, ffn_w_gate, ffn_w_up, ffn_w_down, attn_w_in, attn_b_f, attn_w_out,
           rel_bias, sgu_w_in, sgu_b_in, sgu_ln_g, sgu_ln_b, sgu_w_s, sgu_b_s, sgu_w_out,
           ple_w_proj, ple_w_gate):
    bsz, s, d = x.shape
    x2 = x.reshape(bsz * s, d)
    vec = lambda a: a.reshape(1, -1)
    wg, wu, wd = ffn_w_gate.astype(BF16), ffn_w_up.astype(BF16), ffn_w_down.astype(BF16)
    ln_g4, ln_b4 = ln_g.reshape(DEPTH, 3, 1, d), ln_b.reshape(DEPTH, 3, 1, d)
    p3 = p.reshape(DEPTH, bsz * s, p.shape[-1])
    w_ple_gate, w_ple_proj = ple_w_gate.astype(BF16), ple_w_proj.astype(BF16)
    w_attn_out = attn_w_out.astype(BF16).reshape(attn_w_out.shape[0], 2, W_HEADS, d)
    for i in range(DEPTH):
        j = i // 2
        ln = lambda n: [_pick(ln_g4, i, n), _pick(ln_b4, i, n)]
        ffn_w = lambda n: [_pick(wg, i, n), _pick(wu, i, n), _pick(wd, i, n)] + ln(2 * n)
        x2 = _ffn(x2, ffn_w(0))
        ple = ((p3, i), _pick(w_ple_gate, i), _pick(w_ple_proj, i))
        if i % 2 == 0:
            o_a, o_b = _mixer_attention(x2, bsz, s, attn_w_in[j], attn_b_f[j], rel_bias)
            mix = [o_a, o_b, _pick(w_attn_out, j, 0), _pick(w_attn_out, j, 1)] + ln(1)
            x2 = _ffn(x2, ffn_w(1), mix=mix, ple=ple)
        else:
            x2 = _sgu(x2, sgu_w_in[j].astype(BF16), vec(sgu_b_in[j]), vec(sgu_ln_g[j]),
                      vec(sgu_ln_b[j]), sgu_w_s[j], sgu_b_s[j].T, sgu_w_out[j].astype(BF16),
                      vec(ln_g[i, 1]), vec(ln_b[i, 1]))
            x2 = _ffn(x2, ffn_w(1), ple=ple)
    return x2.reshape(bsz, s, d)
```

```python
import functools
import math

import jax
import jax.numpy as jnp
from jax import lax
from jax.experimental import pallas as pl
from jax.experimental.pallas import tpu as pltpu

D_MODEL = 1024
HEAD_DIM = 64
N_HEADS_A = 8
N_HEADS_B = 8
W_HEADS = N_HEADS_A * HEAD_DIM
MOBA_BLOCK = 256
MOBA_TOPK = 3
REL_BUCKETS = 32
REL_MAX_DIST = 128
SGU_CHUNK = 128
SGU_GROUPS = 8
D_FF = 2816
PLE_DIM = 256
DEPTH = 2
DN_ALPHA = (2.0 * DEPTH) ** 0.25
LN_EPS = 1e-5
NEG_INF = -1e30
LOG2_E = math.log2(math.e)

F32 = jnp.float32
BF16 = jnp.bfloat16

VMEM_LIMIT_BYTES = 56 * 1024 * 1024
LANES = 128
HEADS_PER_TILE = LANES // HEAD_DIM
HEADS_PER_STEP = 2
STEP_LANES = HEADS_PER_STEP * HEAD_DIM
FF_CHUNK = 256
ROW_TILE = 512
ATTN_TILE = MOBA_BLOCK
KV_CHUNK = 1024
BLOCKS_PER_CHUNK = KV_CHUNK // ATTN_TILE
N_BIAS_SLOTS = 3
DECAY_PARTS = 3
GATE_TERMS = 3
ONES_ROWS = 16


def _params(*semantics):
    return pltpu.CompilerParams(dimension_semantics=semantics, vmem_limit_bytes=VMEM_LIMIT_BYTES)


def _resident(shape):
    zeros = (0,) * len(shape)
    return pl.BlockSpec(shape, lambda *_: zeros, pipeline_mode=pl.Buffered(1))


def _pick(array, *lead):
    tail = array.shape[len(lead):]
    index = tuple(lead) + (0,) * len(tail)
    spec = pl.BlockSpec((None,) * len(lead) + tail, lambda *_: index, pipeline_mode=pl.Buffered(1))
    return array, spec


def _layer_norm(y, g, b):
    mu = jnp.mean(y, axis=-1, keepdims=True)
    d = y - mu
    var = jnp.mean(d * d, axis=-1, keepdims=True)
    return d * lax.rsqrt(var + LN_EPS) * g + b


def _dot(a, b):
    return jnp.dot(a, b, preferred_element_type=F32)


def _dot_nt(a, b):
    return lax.dot_general(a, b, (((1,), (1,)), ((), ())), preferred_element_type=F32)


def _ffn_body(x_ref, *refs, with_mix, with_ple):
    refs = list(refs)
    o_ref = refs.pop()
    x = x_ref[...]
    if with_mix:
        oa_ref, ob_ref, wa_ref, wb_ref, mg_ref, mb_ref = refs[:6]
        refs = refs[6:]
        mixed = _dot(oa_ref[...], wa_ref[...]) + _dot(ob_ref[...], wb_ref[...])
        x = _layer_norm(DN_ALPHA * x + mixed, mg_ref[...], mb_ref[...])
    wg_ref, wu_ref, wd_ref, g_ref, b_ref = refs[:5]
    if with_ple:
        p_ref, wpg_ref, wpp_ref = refs[5:]
    xb = x.astype(BF16)
    acc = jnp.zeros(x.shape, F32)
    for c in range(D_FF // FF_CHUNK):
        sl = slice(c * FF_CHUNK, (c + 1) * FF_CHUNK)
        gate = _dot(xb, wg_ref[:, sl])
        up = _dot(xb, wu_ref[:, sl])
        h = (jax.nn.silu(gate) * up).astype(BF16)
        acc = acc + _dot(h, wd_ref[sl, :])
    y = _layer_norm(DN_ALPHA * x + 0.5 * acc, g_ref[...], b_ref[...])
    if with_ple:
        gate = jax.nn.sigmoid(_dot(y.astype(BF16), wpg_ref[...]))
        y = y + gate * _dot(p_ref[...].astype(BF16), wpp_ref[...])
    o_ref[...] = y


def _ffn(x, weights, mix=None, ple=None):
    m, d = x.shape
    tm = min(ROW_TILE, m)
    row = lambda w: pl.BlockSpec((tm, w), lambda i: (i, 0))
    args, in_specs = [x], [row(d)]

    def add(picks):
        for array, spec in picks:
            args.append(array)
            in_specs.append(spec)

    if mix is not None:
        o_a, o_b = mix[:2]
        add([(o_a, row(o_a.shape[1])), (o_b, row(o_b.shape[1]))])
        add(mix[2:])
    add(weights)
    if ple is not None:
        (p, layer), wpg, wpp = ple
        add([(p, pl.BlockSpec((None, tm, p.shape[2]), lambda i: (layer, i, 0))), wpg, wpp])
    return pl.pallas_call(
        functools.partial(_ffn_body, with_mix=mix is not None, with_ple=ple is not None),
        grid=(m // tm,),
        in_specs=in_specs,
        out_specs=row(d),
        out_shape=jax.ShapeDtypeStruct((m, d), F32),
        compiler_params=_params("parallel"),
        name="ffn" + ("_mix" if mix is not None else "") + ("_ple" if ple is not None else ""),
    )(*args)


def _proj_body(x_ref, w_ref, wt_ref, qa_ref, ka_ref, qb_ref, kb_ref, vta_ref, vtb_ref, f_ref):
    xb = x_ref[0].astype(BF16)
    scale = HEAD_DIM ** -0.5 * LOG2_E

    def seg(n):
        return _dot(xb, w_ref[:, n * W_HEADS:(n + 1) * W_HEADS])

    qa_ref[0] = (seg(0) * scale).astype(BF16)
    ka_ref[0] = seg(1).astype(BF16)
    qb_ref[0] = (seg(2) * scale).astype(BF16)
    kb_ref[0] = seg(3).astype(BF16)
    rt = _dot_nt(wt_ref[...], xb)
    for n in range(vta_ref.shape[1]):
        cols = slice(n * KV_CHUNK, (n + 1) * KV_CHUNK)
        vta_ref[0, n] = rt[:W_HEADS, cols].astype(BF16)
        vtb_ref[0, n] = rt[W_HEADS:2 * W_HEADS, cols].astype(BF16)
    f_ref[0] = rt[2 * W_HEADS:, :]


def _attn_proj(x, w_rows, w_t):
    bsz, s, d = x.shape
    tm = min(max(ROW_TILE, KV_CHUNK), s)
    n_chunk = tm // KV_CHUNK
    rows = pl.BlockSpec((1, tm, W_HEADS), lambda b, i: (b, i, 0))
    vts = pl.BlockSpec((1, n_chunk, W_HEADS, KV_CHUNK), lambda b, i: (b, i, 0, 0))
    row_t = jax.ShapeDtypeStruct((bsz, s, W_HEADS), BF16)
    vt_t = jax.ShapeDtypeStruct((bsz, s // KV_CHUNK, W_HEADS, KV_CHUNK), BF16)
    return pl.pallas_call(
        _proj_body,
        grid=(bsz, s // tm),
        in_specs=[pl.BlockSpec((1, tm, d), lambda b, i: (b, i, 0)),
                  _resident(w_rows.shape), _resident(w_t.shape)],
        out_specs=[rows, rows, rows, rows, vts, vts,
                   pl.BlockSpec((1, N_HEADS_B, tm), lambda b, i: (b, 0, i))],
        out_shape=[row_t, row_t, row_t, row_t, vt_t, vt_t,
                   jax.ShapeDtypeStruct((bsz, N_HEADS_B, s), F32)],
        compiler_params=_params("parallel", "parallel"),
        name="attn_proj",
    )(x, w_rows, w_t)


def _decay_body(f_ref, bf_ref, c_ref):
    x = f_ref[0] + bf_ref[...]
    log_f = jnp.minimum(x, 0.0) - jnp.log(1.0 + jnp.exp(-jnp.abs(x)))
    r = lax.broadcasted_iota(jnp.int32, (LANES, LANES), 0)
    c = lax.broadcasted_iota(jnp.int32, (LANES, LANES), 1)
    prefix = (r <= c).astype(BF16)
    carry = jnp.zeros((x.shape[0], 1), F32)
    for n in range(x.shape[1] // LANES):
        sl = slice(n * LANES, (n + 1) * LANES)
        rest = log_f[:, sl]
        within = jnp.zeros(rest.shape, F32)
        for _ in range(DECAY_PARTS):
            part = rest.astype(BF16)
            rest = rest - part.astype(F32)
            within = within + _dot(part, prefix)
        cs = within + carry
        carry = cs[:, LANES - 1:LANES]
        rest = cs * LOG2_E
        for part in range(DECAY_PARTS):
            term = rest.astype(BF16).astype(F32)
            c_ref[0, part, :, sl] = term
            rest = rest - term


def _fox_decay(f_t, b_f):
    bsz, h, s = f_t.shape
    return pl.pallas_call(
        _decay_body,
        grid=(bsz,),
        in_specs=[pl.BlockSpec((1, h, s), lambda b: (b, 0, 0)), _resident((h, 1))],
        out_specs=pl.BlockSpec((1, DECAY_PARTS, h, s), lambda b: (b, 0, 0, 0)),
        out_shape=jax.ShapeDtypeStruct((bsz, DECAY_PARTS, h, s), F32),
        compiler_params=_params("parallel"),
        name="fox_decay",
    )(f_t, b_f.reshape(h, 1))


def _bias_body(tbl_ref, o_ref):
    h = pl.program_id(0)
    r = lax.broadcasted_iota(jnp.int32, (MOBA_BLOCK, MOBA_BLOCK), 0)
    t = lax.broadcasted_iota(jnp.int32, (MOBA_BLOCK, MOBA_BLOCK), 1)
    max_exact = REL_BUCKETS // 2
    for slot in range(N_BIAS_SLOTS):
        n = jnp.maximum(t - r + slot * MOBA_BLOCK, 0)
        nf = jnp.maximum(n, max_exact).astype(F32)
        large = max_exact + (jnp.log(nf / max_exact) / math.log(REL_MAX_DIST / max_exact)
                             * (REL_BUCKETS - max_exact)).astype(jnp.int32)
        large = jnp.minimum(large, REL_BUCKETS - 1)
        bucket = jnp.where(n < max_exact, n, large)
        bias = jnp.zeros((MOBA_BLOCK, MOBA_BLOCK), F32)
        for k in range(REL_BUCKETS):
            bias = jnp.where(bucket == k, tbl_ref[k, h], bias)
        bias = bias * LOG2_E
        if slot == 0:
            bias = jnp.where(t >= r, bias, NEG_INF)
        o_ref[0, slot] = bias


def _moba_bias(rel_bias):
    shape = (N_HEADS_A, N_BIAS_SLOTS, MOBA_BLOCK, MOBA_BLOCK)
    return pl.pallas_call(
        _bias_body,
        grid=(N_HEADS_A,),
        in_specs=[pl.BlockSpec(memory_space=pltpu.SMEM)],
        out_specs=pl.BlockSpec((1,) + shape[1:], lambda h: (h, 0, 0, 0)),
        out_shape=jax.ShapeDtypeStruct(shape, F32),
        compiler_params=_params("parallel"),
        name="moba_bias",
    )(rel_bias)


def _tile_lanes(head):
    tile = head // HEADS_PER_TILE
    return slice(tile * LANES, (tile + 1) * LANES)


def _lane_tile(x, head):
    return x[:, _tile_lanes(head)]


def _head_only(x, head):
    xt = _lane_tile(x, head)
    lane = lax.broadcasted_iota(jnp.int32, xt.shape, 1)
    return jnp.where(lane // HEAD_DIM == head % HEADS_PER_TILE, xt, jnp.zeros_like(xt))


def _chunk_rows(c):
    return pl.ds(pl.multiple_of(c * KV_CHUNK, KV_CHUNK), KV_CHUNK)


STREAMS = [(h, tile) for h in range(HEADS_PER_STEP) for tile in range(BLOCKS_PER_CHUNK)]


def _diag_keys(tile):
    return (tile + 1) * ATTN_TILE


def _flash_scratch():
    n = len(STREAMS)
    score_buf = pltpu.VMEM((n, KV_CHUNK, ATTN_TILE), F32)
    max_buf = pltpu.VMEM((n, 1, ATTN_TILE), F32)
    return [score_buf] * 3 + [max_buf] * 3 + [
        pltpu.VMEM((n, 1, ATTN_TILE), F32),
        pltpu.VMEM((n, HEAD_DIM + ONES_ROWS, ATTN_TILE), F32)]


def _next_step_spec(n_steps):
    return pl.BlockSpec((1, KV_CHUNK, STEP_LANES),
                        lambda b, p, g: (b, jnp.minimum(g + 1, n_steps - 1), p))


def _flash_step(g, n_steps, prepare, scorer, q_ref, qnext_ref, vt_ref, o_ref,
                s0_ref, s1_ref, sd_ref, mx0_ref, mx1_ref, mxd_ref, m_ref, acc_ref):
    diag_buf = 2
    bufs = ((s0_ref, mx0_ref), (s1_ref, mx1_ref), (sd_ref, mxd_ref))
    slot = g % 2
    ones = jnp.ones((ONES_ROWS, KV_CHUNK), BF16)

    def issue(score_fn, n, c, kind, buf):
        s_ref, mx_ref = bufs[buf]
        s = score_fn(n, c, kind)
        s_ref[n, :s.shape[0]] = s
        mx_ref[n] = jnp.max(s, axis=0, keepdims=True)

    def absorb(n, c, buf, diag=False):
        s_ref, mx_ref = bufs[buf]
        h, tile = STREAMS[n]
        keys = _diag_keys(tile) if diag else KV_CHUNK
        m = m_ref[n]
        m_new = jnp.maximum(m, mx_ref[n])
        p = jnp.exp2(s_ref[n, :keys] - m_new).astype(BF16)
        vt = jnp.concatenate([vt_ref[0, c, h * HEAD_DIM:(h + 1) * HEAD_DIM, :keys],
                              ones[:, :keys]], axis=0)
        acc_ref[n] = jnp.exp2(m - m_new) * acc_ref[n] + _dot(vt, p)
        m_ref[n] = m_new

    def overlap(score_fn, c_issue, kind, buf_issue, c_absorb, buf_absorb, diag=False):
        for n in range(len(STREAMS)):
            issue(score_fn, n, c_issue, kind, buf_issue)
            absorb(n, c_absorb, buf_absorb, diag)

    def start_step(step, q_rows_ref, step_slot, c_absorb=None, buf_absorb=None):
        prepare(step, q_rows_ref, step_slot)
        score_fn = scorer(step, q_rows_ref, step_slot)
        if c_absorb is None:
            for n in range(len(STREAMS)):
                issue(score_fn, n, step, "diag", diag_buf)
        else:
            overlap(score_fn, step, "diag", diag_buf, c_absorb, buf_absorb)

    def start_next(c_absorb=None, buf_absorb=None):
        start_step(jnp.minimum(g + 1, n_steps - 1), qnext_ref, 1 - slot, c_absorb, buf_absorb)

    @pl.when(g == 0)
    def _():
        start_step(g, q_ref, slot)

    m_ref[...] = jnp.full(m_ref.shape, NEG_INF, F32)
    acc_ref[...] = jnp.zeros(acc_ref.shape, F32)
    score_fn = scorer(g, q_ref, slot)

    @pl.when(g == 0)
    def _():
        for n in range(len(STREAMS)):
            absorb(n, g, diag_buf, diag=True)
        start_next()

    @pl.when(g > 0)
    def _():
        overlap(score_fn, g - 1, "near", 1, g, diag_buf, diag=True)
        n_far = g - 1

        def pair(n, carry):
            c = g - 2 - 2 * n
            overlap(score_fn, c, "far", 0, c + 1, 1)
            overlap(score_fn, c - 1, "far", 1, c, 0)
            return carry

        lax.fori_loop(0, n_far // 2, pair, 0)

        @pl.when(n_far % 2 == 1)
        def _():
            overlap(score_fn, 0, "far", 0, 1, 1)
            start_next(0, 0)

        @pl.when(n_far % 2 == 0)
        def _():
            start_next(0, 1)

    for tile in range(BLOCKS_PER_CHUNK):
        outs = [acc_ref[n, :HEAD_DIM] / acc_ref[n, HEAD_DIM:HEAD_DIM + 1]
                for n, (_, stream_tile) in enumerate(STREAMS) if stream_tile == tile]
        o_ref[0, tile * ATTN_TILE:(tile + 1) * ATTN_TILE, :] = (
            jnp.concatenate(outs, axis=0).T.astype(o_ref.dtype))


def _fox_body(q_ref, qnext_ref, k_ref, cp_ref, vt_ref, o_ref, mask_ref, *flash_refs):
    g = pl.program_id(2)

    @pl.when(g == 0)
    def _():
        r = lax.broadcasted_iota(jnp.int32, (KV_CHUNK, ATTN_TILE), 0)
        t = lax.broadcasted_iota(jnp.int32, (KV_CHUNK, ATTN_TILE), 1)
        for tile in range(BLOCKS_PER_CHUNK):
            mask_ref[tile] = jnp.where(r <= t + tile * ATTN_TILE, 0.0, NEG_INF)

    def scorer(step, q_rows_ref, slot):
        lane = lax.broadcasted_iota(jnp.int32, (ATTN_TILE, LANES), 1)
        qxs = []
        for h, tile in STREAMS:
            q = q_rows_ref[0, tile * ATTN_TILE:(tile + 1) * ATTN_TILE, :]
            on_terms = (lane >= h * DECAY_PARTS) & (lane < (h + 1) * DECAY_PARTS)
            minus_one = jnp.where(on_terms, -1.0, 0.0).astype(BF16)
            qxs.append(jnp.concatenate([_head_only(q, h), minus_one], axis=1))

        def score_fn(n, c, kind):
            h, tile = STREAMS[n]
            rows = _chunk_rows(c)
            kx = jnp.concatenate([k_ref[0, rows, _tile_lanes(h)], cp_ref[0, 0, rows, :]], axis=1)
            if kind == "diag":
                keys = _diag_keys(tile)
                return _dot_nt(kx[:keys], qxs[n]) + mask_ref[tile, :keys]
            return _dot_nt(kx, qxs[n])

        return score_fn

    _flash_step(g, pl.num_programs(2), lambda step, q_rows_ref, slot: None, scorer, q_ref,
                qnext_ref, vt_ref, o_ref, *flash_refs)


def _moba_body(q_ref, qnext_ref, k_ref, vt_ref, bias_ref, o_ref, kbar_ref, kterms_ref, sel_ref,
               far_ref, *flash_refs):
    g = pl.program_id(2)
    n_blk = kbar_ref.shape[0]

    @pl.when(g == 0)
    def _():
        for n in range(n_blk):
            kb = k_ref[0, n * MOBA_BLOCK:(n + 1) * MOBA_BLOCK, :].astype(F32)
            kbar_ref[n:n + 1, :] = jnp.mean(kb, axis=0, keepdims=True)
        rest = kbar_ref[...]
        lane = lax.broadcasted_iota(jnp.int32, (n_blk, LANES), 1)
        for term in range(GATE_TERMS):
            part = rest.astype(BF16)
            rest = rest - part.astype(F32)
            for tile in range(STEP_LANES // LANES):
                part_tile = part[:, tile * LANES:(tile + 1) * LANES]
                for h in range(HEADS_PER_TILE):
                    row = (h * GATE_TERMS + term) * n_blk
                    kterms_ref[tile, row:row + n_blk, :] = jnp.where(
                        lane // HEAD_DIM == h, part_tile, jnp.zeros_like(part_tile))

    def q_tile(q_rows_ref, tile):
        return q_rows_ref[0, tile * ATTN_TILE:(tile + 1) * ATTN_TILE, :]

    def prepare(step, q_rows_ref, slot):
        blk = lax.broadcasted_iota(jnp.int32, (n_blk, ATTN_TILE), 0)
        tile_gates = {}
        for n, (h, tile) in enumerate(STREAMS):
            t = step * BLOCKS_PER_CHUNK + tile
            lanes = h // HEADS_PER_TILE
            if (lanes, tile) not in tile_gates:
                tile_gates[lanes, tile] = _dot_nt(kterms_ref[lanes],
                                                  _lane_tile(q_tile(q_rows_ref, tile), h))
            row = (h % HEADS_PER_TILE) * GATE_TERMS * n_blk
            gates = tile_gates[lanes, tile]
            gate = gates[row:row + n_blk]
            for term in range(1, GATE_TERMS):
                gate = gate + gates[row + term * n_blk:row + (term + 1) * n_blk]
            gate = jnp.where(blk < t, gate, NEG_INF)
            keep = blk < 0
            for _ in range(MOBA_TOPK):
                top = jnp.max(gate, axis=0, keepdims=True)
                first = jnp.min(jnp.where(gate == top, blk, n_blk), axis=0, keepdims=True)
                hit = blk == first
                keep = keep | (hit & (blk < t))
                gate = jnp.where(hit, -jnp.inf, gate)
            sel = jnp.where(keep, 0.0, NEG_INF)
            sel_ref[slot, n] = sel
            far_ref[slot, n] = sel + bias_ref[h, N_BIAS_SLOTS - 1, 0:1, 0:1]

    def scorer(step, q_rows_ref, slot):
        qms = [_head_only(q_tile(q_rows_ref, tile), h) for h, tile in STREAMS]

        def score_fn(n, c, kind):
            h, tile = STREAMS[n]
            chunks_back = {"diag": 0, "near": 1, "far": 2}[kind]
            keys = _diag_keys(tile) if kind == "diag" else KV_CHUNK
            s = _dot_nt(k_ref[0, _chunk_rows(c), _tile_lanes(h)][:keys], qms[n])
            parts = []
            for b in range(keys // ATTN_TILE):
                j = c * BLOCKS_PER_CHUNK + b
                scores = s[b * ATTN_TILE:(b + 1) * ATTN_TILE]
                back = chunks_back * BLOCKS_PER_CHUNK + tile - b
                if back == 0:
                    parts.append(scores + bias_ref[h, 0])
                elif back == 1:
                    parts.append(scores + bias_ref[h, 1] + sel_ref[slot, n, pl.ds(j, 1), :])
                else:
                    parts.append(scores + far_ref[slot, n, pl.ds(j, 1), :])
            return jnp.concatenate(parts, axis=0)

        return score_fn

    _flash_step(g, pl.num_programs(2), prepare, scorer, q_ref, qnext_ref, vt_ref, o_ref,
                *flash_refs)


def _attn_specs(bsz, s):
    n_blk = s // KV_CHUNK
    q_spec = pl.BlockSpec((1, KV_CHUNK, STEP_LANES), lambda b, p, i: (b, i, p))
    once = pl.Buffered(1)
    k_spec = pl.BlockSpec((1, s, STEP_LANES), lambda b, p, i: (b, 0, p), pipeline_mode=once)
    vt_spec = pl.BlockSpec((1, s // KV_CHUNK, STEP_LANES, KV_CHUNK), lambda b, p, i: (b, 0, p, 0),
                           pipeline_mode=once)
    out_shape = jax.ShapeDtypeStruct((bsz, s, W_HEADS), BF16)
    grid = (bsz, W_HEADS // STEP_LANES, n_blk)
    return grid, q_spec, k_spec, vt_spec, out_shape


def _fox_attention(q, k, decay_terms, vt):
    bsz, s, _ = q.shape
    grid, q_spec, k_spec, vt_spec, out_shape = _attn_specs(bsz, s)
    return pl.pallas_call(
        _fox_body,
        grid=grid,
        in_specs=[q_spec, _next_step_spec(grid[2]), k_spec,
                  pl.BlockSpec((1, 1, s, LANES), lambda b, p, i: (b, p, 0, 0),
                               pipeline_mode=pl.Buffered(1)),
                  vt_spec],
        out_specs=q_spec,
        out_shape=out_shape,
        scratch_shapes=[pltpu.VMEM((BLOCKS_PER_CHUNK, KV_CHUNK, ATTN_TILE), F32)]
        + _flash_scratch(),
        compiler_params=_params("parallel", "parallel", "arbitrary"),
        name="fox_attention",
    )(q, q, k, decay_terms, vt)


def _moba_attention(q, k, vt, bias):
    bsz, s, _ = q.shape
    grid, q_spec, k_spec, vt_spec, out_shape = _attn_specs(bsz, s)
    n_blk = s // MOBA_BLOCK
    return pl.pallas_call(
        _moba_body,
        grid=grid,
        in_specs=[q_spec, _next_step_spec(grid[2]), k_spec, vt_spec,
                  pl.BlockSpec((HEADS_PER_STEP, N_BIAS_SLOTS, MOBA_BLOCK, MOBA_BLOCK),
                               lambda b, p, i: (p, 0, 0, 0), pipeline_mode=pl.Buffered(1))],
        out_specs=q_spec,
        out_shape=out_shape,
        scratch_shapes=[pltpu.VMEM((n_blk, STEP_LANES), F32),
                        pltpu.VMEM((STEP_LANES // LANES, HEADS_PER_TILE * GATE_TERMS * n_blk, LANES),
                                   BF16),
                        pltpu.VMEM((2, len(STREAMS), n_blk, ATTN_TILE), F32),
                        pltpu.VMEM((2, len(STREAMS), n_blk, ATTN_TILE), F32)] + _flash_scratch(),
        compiler_params=_params("parallel", "parallel", "arbitrary"),
        name="moba_attention",
    )(q, q, k, vt, bias)


def _sgu_body(x_ref, win_ref, bin_ref, lg_ref, lb_ref, ws_ref, bs_ref, wout_ref, g_ref, b_ref,
              o_ref, y_ref):
    x = x_ref[...]
    xb = x.astype(BF16)
    w = x.shape[1]
    u = jax.nn.gelu(_dot(xb, win_ref[:, :w]) + bin_ref[:, :w])
    v = jax.nn.gelu(_dot(xb, win_ref[:, w:]) + bin_ref[:, w:])
    vb = _layer_norm(v, lg_ref[...], lb_ref[...]).astype(BF16)
    t = lax.broadcasted_iota(jnp.int32, (SGU_CHUNK, SGU_CHUNK), 0)
    s = lax.broadcasted_iota(jnp.int32, (SGU_CHUNK, SGU_CHUNK), 1)
    gw = w // SGU_GROUPS
    for g in range(SGU_GROUPS):
        w_g = jnp.where(t >= s, ws_ref[g], 0.0).astype(BF16)
        cols = slice(g * gw, (g + 1) * gw)
        for c in range(x.shape[0] // SGU_CHUNK):
            rows = slice(c * SGU_CHUNK, (c + 1) * SGU_CHUNK)
            mixed = _dot(w_g, vb[rows, cols]) + bs_ref[:, g:g + 1]
            y_ref[rows, cols] = (u[rows, cols] * mixed).astype(BF16)
    out = _dot(y_ref[...], wout_ref[...])
    o_ref[...] = _layer_norm(DN_ALPHA * x + out, g_ref[...], b_ref[...])


def _sgu(x, w_in, b_in, ln_g, ln_b, w_s, b_s_t, w_out, g, b):
    m, d = x.shape
    tm = min(ROW_TILE, m)
    row = pl.BlockSpec((tm, d), lambda i: (i, 0))
    weights = [w_in, b_in, ln_g, ln_b, w_s, b_s_t, w_out, g, b]
    return pl.pallas_call(
        _sgu_body,
        grid=(m // tm,),
        in_specs=[row] + [_resident(a.shape) for a in weights],
        out_specs=row,
        out_shape=jax.ShapeDtypeStruct((m, d), F32),
        scratch_shapes=[pltpu.VMEM((tm, d), BF16)],
        compiler_params=_params("parallel"),
        name="sgu",
    )(x, *weights)


def _mixer_attention(x2, bsz, s, w_in, b_f, rel_bias):
    d = x2.shape[1]
    seg = lambda n: w_in[:, n * W_HEADS:(n + 1) * W_HEADS]
    w_rows = jnp.concatenate([seg(0), seg(1), seg(3), seg(4)], axis=1).astype(BF16)
    w_t = jnp.concatenate([seg(2), seg(5), w_in[:, 6 * W_HEADS:]], axis=1).T.astype(BF16)
    qa, ka, qb, kb, vta, vtb, f_t = _attn_proj(x2.reshape(bsz, s, d), w_rows, w_t)
    n_groups = N_HEADS_B // HEADS_PER_STEP
    terms = _fox_decay(f_t, b_f).reshape(bsz, DECAY_PARTS, n_groups, HEADS_PER_STEP, s)
    terms = terms.transpose(0, 2, 4, 3, 1).reshape(bsz, n_groups, s, HEADS_PER_STEP * DECAY_PARTS)
    terms = jnp.pad(terms.astype(BF16), ((0, 0),) * 3 + ((0, LANES - terms.shape[-1]),))
    o_a = _moba_attention(qa, ka, vta, _moba_bias(rel_bias))
    o_b = _fox_attention(qb, kb, terms, vtb)
    return o_a.reshape(bsz * s, W_HEADS), o_b.reshape(bsz * s, W_HEADS)


def kernel(x, p, ln_g, ln_b, ffn_w_gate, ffn_w_up, ffn_w_down, attn_w_in, attn_b_f, attn_w_out,
           rel_bias, sgu_w_in, sgu_b_in, sgu_ln_g, sgu_ln_b, sgu_w_s, sgu_b_s, sgu_w_out,
           ple_w_proj, ple_w_gate):
    bsz, s, d = x.shape
    x2 = x.reshape(bsz * s, d)
    vec = lambda a: a.reshape(1, -1)
    wg, wu, wd = ffn_w_gate.astype(BF16), ffn_w_up.astype(BF16), ffn_w_down.astype(BF16)
    ln_g4, ln_b4 = ln_g.reshape(DEPTH, 3, 1, d), ln_b.reshape(DEPTH, 3, 1, d)
    p3 = p.reshape(DEPTH, bsz * s, p.shape[-1])
    w_ple_gate, w_ple_proj = ple_w_gate.astype(BF16), ple_w_proj.astype(BF16)
    w_attn_out = attn_w_out.astype(BF16).reshape(attn_w_out.shape[0], 2, W_HEADS, d)
    for i in range(DEPTH):
        j = i // 2
        ln = lambda n: [_pick(ln_g4, i, n), _pick(ln_b4, i, n)]
        ffn_w = lambda n: [_pick(wg, i, n), _pick(wu, i, n), _pick(wd, i, n)] + ln(2 * n)
        x2 = _ffn(x2, ffn_w(0))
        ple = ((p3, i), _pick(w_ple_gate, i), _pick(w_ple_proj, i))
        if i % 2 == 0:
            o_a, o_b = _mixer_attention(x2, bsz, s, attn_w_in[j], attn_b_f[j], rel_bias)
            mix = [o_a, o_b, _pick(w_attn_out, j, 0), _pick(w_attn_out, j, 1)] + ln(1)
            x2 = _ffn(x2, ffn_w(1), mix=mix, ple=ple)
        else:
            x2 = _sgu(x2, sgu_w_in[j].astype(BF16), vec(sgu_b_in[j]), vec(sgu_ln_g[j]),
                      vec(sgu_ln_b[j]), sgu_w_s[j], sgu_b_s[j].T, sgu_w_out[j].astype(BF16),
                      vec(ln_g[i, 1]), vec(ln_b[i, 1]))
            x2 = _ffn(x2, ffn_w(1), ple=ple)
    return x2.reshape(bsz, s, d)
```

```python
import functools
import math

import jax
import jax.numpy as jnp
from jax import lax
from jax.experimental import pallas as pl
from jax.experimental.pallas import tpu as pltpu

D_MODEL = 1024
HEAD_DIM = 64
N_HEADS_A = 8
N_HEADS_B = 8
W_HEADS = N_HEADS_A * HEAD_DIM
MOBA_BLOCK = 256
MOBA_TOPK = 3
REL_BUCKETS = 32
REL_MAX_DIST = 128
SGU_CHUNK = 128
SGU_GROUPS = 8
D_FF = 2816
PLE_DIM = 256
DEPTH = 2
DN_ALPHA = (2.0 * DEPTH) ** 0.25
LN_EPS = 1e-5
NEG_INF = -1e30
LOG2_E = math.log2(math.e)

F32 = jnp.float32
BF16 = jnp.bfloat16

VMEM_LIMIT_BYTES = 56 * 1024 * 1024
LANES = 128
HEADS_PER_TILE = LANES // HEAD_DIM
HEADS_PER_STEP = 2
STEP_LANES = HEADS_PER_STEP * HEAD_DIM
FF_CHUNK = 256
ROW_TILE = 1024
ATTN_TILE = MOBA_BLOCK
KV_CHUNK = 1024
BLOCKS_PER_CHUNK = KV_CHUNK // ATTN_TILE
N_BIAS_SLOTS = 3
DECAY_PARTS = 3
GATE_TERMS = 3
ONES_ROWS = 16


def _params(*semantics):
    return pltpu.CompilerParams(dimension_semantics=semantics, vmem_limit_bytes=VMEM_LIMIT_BYTES)


def _resident(shape):
    zeros = (0,) * len(shape)
    return pl.BlockSpec(shape, lambda *_: zeros, pipeline_mode=pl.Buffered(1))


def _pick(array, *lead):
    tail = array.shape[len(lead):]
    index = tuple(lead) + (0,) * len(tail)
    spec = pl.BlockSpec((None,) * len(lead) + tail, lambda *_: index, pipeline_mode=pl.Buffered(1))
    return array, spec


def _layer_norm(y, g, b):
    mu = jnp.mean(y, axis=-1, keepdims=True)
    d = y - mu
    var = jnp.mean(d * d, axis=-1, keepdims=True)
    return d * lax.rsqrt(var + LN_EPS) * g + b


def _dot(a, b):
    return jnp.dot(a, b, preferred_element_type=F32)


def _dot_nt(a, b):
    return lax.dot_general(a, b, (((1,), (1,)), ((), ())), preferred_element_type=F32)


def _ffn_body(x_ref, *refs, with_mix, with_ple):
    refs = list(refs)
    o_ref = refs.pop()
    x = x_ref[...]
    if with_mix:
        oa_ref, ob_ref, wa_ref, wb_ref, mg_ref, mb_ref = refs[:6]
        refs = refs[6:]
        mixed = _dot(oa_ref[...], wa_ref[...]) + _dot(ob_ref[...], wb_ref[...])
        x = _layer_norm(DN_ALPHA * x + mixed, mg_ref[...], mb_ref[...])
    wg_ref, wu_ref, wd_ref, g_ref, b_ref = refs[:5]
    if with_ple:
        p_ref, wpg_ref, wpp_ref = refs[5:]
    xb = x.astype(BF16)
    acc = jnp.zeros(x.shape, F32)
    for c in range(D_FF // FF_CHUNK):
        sl = slice(c * FF_CHUNK, (c + 1) * FF_CHUNK)
        gate = _dot(xb, wg_ref[:, sl])
        up = _dot(xb, wu_ref[:, sl])
        h = (jax.nn.silu(gate) * up).astype(BF16)
        acc = acc + _dot(h, wd_ref[sl, :])
    y = _layer_norm(DN_ALPHA * x + 0.5 * acc, g_ref[...], b_ref[...])
    if with_ple:
        gate = jax.nn.sigmoid(_dot(y.astype(BF16), wpg_ref[...]))
        y = y + gate * _dot(p_ref[...].astype(BF16), wpp_ref[...])
    o_ref[...] = y


def _ffn(x, weights, mix=None, ple=None):
    m, d = x.shape
    tm = min(ROW_TILE, m)
    row = lambda w: pl.BlockSpec((tm, w), lambda i: (i, 0))
    args, in_specs = [x], [row(d)]

    def add(picks):
        for array, spec in picks:
            args.append(array)
            in_specs.append(spec)

    if mix is not None:
        o_a, o_b = mix[:2]
        add([(o_a, row(o_a.shape[1])), (o_b, row(o_b.shape[1]))])
        add(mix[2:])
    add(weights)
    if ple is not None:
        (p, layer), wpg, wpp = ple
        add([(p, pl.BlockSpec((None, tm, p.shape[2]), lambda i: (layer, i, 0))), wpg, wpp])
    return pl.pallas_call(
        functools.partial(_ffn_body, with_mix=mix is not None, with_ple=ple is not None),
        grid=(m // tm,),
        in_specs=in_specs,
        out_specs=row(d),
        out_shape=jax.ShapeDtypeStruct((m, d), F32),
        compiler_params=_params("parallel"),
        name="ffn" + ("_mix" if mix is not None else "") + ("_ple" if ple is not None else ""),
    )(*args)


def _proj_body(x_ref, w_ref, wt_ref, qa_ref, ka_ref, qb_ref, kb_ref, vta_ref, vtb_ref, f_ref):
    xb = x_ref[0].astype(BF16)
    scale = HEAD_DIM ** -0.5 * LOG2_E

    def seg(n):
        return _dot(xb, w_ref[:, n * W_HEADS:(n + 1) * W_HEADS])

    qa_ref[0] = (seg(0) * scale).astype(BF16)
    ka_ref[0] = seg(1).astype(BF16)
    qb_ref[0] = (seg(2) * scale).astype(BF16)
    kb_ref[0] = seg(3).astype(BF16)
    rt = _dot_nt(wt_ref[...], xb)
    for n in range(vta_ref.shape[1]):
        cols = slice(n * KV_CHUNK, (n + 1) * KV_CHUNK)
        vta_ref[0, n] = rt[:W_HEADS, cols].astype(BF16)
        vtb_ref[0, n] = rt[W_HEADS:2 * W_HEADS, cols].astype(BF16)
    f_ref[0] = rt[2 * W_HEADS:, :]


def _attn_proj(x, w_rows, w_t):
    bsz, s, d = x.shape
    tm = min(max(ROW_TILE, KV_CHUNK), s)
    n_chunk = tm // KV_CHUNK
    rows = pl.BlockSpec((1, tm, W_HEADS), lambda b, i: (b, i, 0))
    vts = pl.BlockSpec((1, n_chunk, W_HEADS, KV_CHUNK), lambda b, i: (b, i, 0, 0))
    row_t = jax.ShapeDtypeStruct((bsz, s, W_HEADS), BF16)
    vt_t = jax.ShapeDtypeStruct((bsz, s // KV_CHUNK, W_HEADS, KV_CHUNK), BF16)
    return pl.pallas_call(
        _proj_body,
        grid=(bsz, s // tm),
        in_specs=[pl.BlockSpec((1, tm, d), lambda b, i: (b, i, 0)),
                  _resident(w_rows.shape), _resident(w_t.shape)],
        out_specs=[rows, rows, rows, rows, vts, vts,
                   pl.BlockSpec((1, N_HEADS_B, tm), lambda b, i: (b, 0, i))],
        out_shape=[row_t, row_t, row_t, row_t, vt_t, vt_t,
                   jax.ShapeDtypeStruct((bsz, N_HEADS_B, s), F32)],
        compiler_params=_params("parallel", "parallel"),
        name="attn_proj",
    )(x, w_rows, w_t)


def _decay_body(f_ref, bf_ref, c_ref):
    x = f_ref[0] + bf_ref[...]
    log_f = jnp.minimum(x, 0.0) - jnp.log(1.0 + jnp.exp(-jnp.abs(x)))
    r = lax.broadcasted_iota(jnp.int32, (LANES, LANES), 0)
    c = lax.broadcasted_iota(jnp.int32, (LANES, LANES), 1)
    prefix = (r <= c).astype(BF16)
    carry = jnp.zeros((x.shape[0], 1), F32)
    for n in range(x.shape[1] // LANES):
        sl = slice(n * LANES, (n + 1) * LANES)
        rest = log_f[:, sl]
        within = jnp.zeros(rest.shape, F32)
        for _ in range(DECAY_PARTS):
            part = rest.astype(BF16)
            rest = rest - part.astype(F32)
            within = within + _dot(part, prefix)
        cs = within + carry
        carry = cs[:, LANES - 1:LANES]
        rest = cs * LOG2_E
        for part in range(DECAY_PARTS):
            term = rest.astype(BF16).astype(F32)
            c_ref[0, part, :, sl] = term
            rest = rest - term


def _fox_decay(f_t, b_f):
    bsz, h, s = f_t.shape
    return pl.pallas_call(
        _decay_body,
        grid=(bsz,),
        in_specs=[pl.BlockSpec((1, h, s), lambda b: (b, 0, 0)), _resident((h, 1))],
        out_specs=pl.BlockSpec((1, DECAY_PARTS, h, s), lambda b: (b, 0, 0, 0)),
        out_shape=jax.ShapeDtypeStruct((bsz, DECAY_PARTS, h, s), F32),
        compiler_params=_params("parallel"),
        name="fox_decay",
    )(f_t, b_f.reshape(h, 1))


def _bias_body(tbl_ref, o_ref):
    h = pl.program_id(0)
    r = lax.broadcasted_iota(jnp.int32, (MOBA_BLOCK, MOBA_BLOCK), 0)
    t = lax.broadcasted_iota(jnp.int32, (MOBA_BLOCK, MOBA_BLOCK), 1)
    max_exact = REL_BUCKETS // 2
    for slot in range(N_BIAS_SLOTS):
        n = jnp.maximum(t - r + slot * MOBA_BLOCK, 0)
        nf = jnp.maximum(n, max_exact).astype(F32)
        large = max_exact + (jnp.log(nf / max_exact) / math.log(REL_MAX_DIST / max_exact)
                             * (REL_BUCKETS - max_exact)).astype(jnp.int32)
        large = jnp.minimum(large, REL_BUCKETS - 1)
        bucket = jnp.where(n < max_exact, n, large)
        bias = jnp.zeros((MOBA_BLOCK, MOBA_BLOCK), F32)
        for k in range(REL_BUCKETS):
            bias = jnp.where(bucket == k, tbl_ref[k, h], bias)
        bias = bias * LOG2_E
        if slot == 0:
            bias = jnp.where(t >= r, bias, NEG_INF)
        o_ref[0, slot] = bias


def _moba_bias(rel_bias):
    shape = (N_HEADS_A, N_BIAS_SLOTS, MOBA_BLOCK, MOBA_BLOCK)
    return pl.pallas_call(
        _bias_body,
        grid=(N_HEADS_A,),
        in_specs=[pl.BlockSpec(memory_space=pltpu.SMEM)],
        out_specs=pl.BlockSpec((1,) + shape[1:], lambda h: (h, 0, 0, 0)),
        out_shape=jax.ShapeDtypeStruct(shape, F32),
        compiler_params=_params("parallel"),
        name="moba_bias",
    )(rel_bias)


def _tile_lanes(head):
    tile = head // HEADS_PER_TILE
    return slice(tile * LANES, (tile + 1) * LANES)


def _lane_tile(x, head):
    return x[:, _tile_lanes(head)]


def _head_only(x, head):
    xt = _lane_tile(x, head)
    lane = lax.broadcasted_iota(jnp.int32, xt.shape, 1)
    return jnp.where(lane // HEAD_DIM == head % HEADS_PER_TILE, xt, jnp.zeros_like(xt))


def _chunk_rows(c):
    return pl.ds(pl.multiple_of(c * KV_CHUNK, KV_CHUNK), KV_CHUNK)


STREAMS = [(h, tile) for h in range(HEADS_PER_STEP) for tile in range(BLOCKS_PER_CHUNK)]


def _diag_keys(tile):
    return (tile + 1) * ATTN_TILE


def _flash_scratch():
    n = len(STREAMS)
    score_buf = pltpu.VMEM((n, KV_CHUNK, ATTN_TILE), F32)
    max_buf = pltpu.VMEM((n, 1, ATTN_TILE), F32)
    return [score_buf] * 3 + [max_buf] * 3 + [
        pltpu.VMEM((n, 1, ATTN_TILE), F32),
        pltpu.VMEM((n, HEAD_DIM + ONES_ROWS, ATTN_TILE), F32)]


def _next_step_spec(n_steps):
    return pl.BlockSpec((1, KV_CHUNK, STEP_LANES),
                        lambda b, p, g: (b, jnp.minimum(g + 1, n_steps - 1), p))


def _flash_step(g, n_steps, prepare, scorer, q_ref, qnext_ref, vt_ref, o_ref,
                s0_ref, s1_ref, sd_ref, mx0_ref, mx1_ref, mxd_ref, m_ref, acc_ref):
    diag_buf = 2
    bufs = ((s0_ref, mx0_ref), (s1_ref, mx1_ref), (sd_ref, mxd_ref))
    slot = g % 2
    ones = jnp.ones((ONES_ROWS, KV_CHUNK), BF16)

    def issue(score_fn, n, c, kind, buf):
        s_ref, mx_ref = bufs[buf]
        s = score_fn(n, c, kind)
        s_ref[n, :s.shape[0]] = s
        mx_ref[n] = jnp.max(s, axis=0, keepdims=True)

    def absorb(n, c, buf, diag=False):
        s_ref, mx_ref = bufs[buf]
        h, tile = STREAMS[n]
        keys = _diag_keys(tile) if diag else KV_CHUNK
        m = m_ref[n]
        m_new = jnp.maximum(m, mx_ref[n])
        p = jnp.exp2(s_ref[n, :keys] - m_new).astype(BF16)
        vt = jnp.concatenate([vt_ref[0, c, h * HEAD_DIM:(h + 1) * HEAD_DIM, :keys],
                              ones[:, :keys]], axis=0)
        acc_ref[n] = jnp.exp2(m - m_new) * acc_ref[n] + _dot(vt, p)
        m_ref[n] = m_new

    def overlap(score_fn, c_issue, kind, buf_issue, c_absorb, buf_absorb, diag=False):
        for n in range(len(STREAMS)):
            issue(score_fn, n, c_issue, kind, buf_issue)
            absorb(n, c_absorb, buf_absorb, diag)

    def start_step(step, q_rows_ref, step_slot, c_absorb=None, buf_absorb=None):
        prepare(step, q_rows_ref, step_slot)
        score_fn = scorer(step, q_rows_ref, step_slot)
        if c_absorb is None:
            for n in range(len(STREAMS)):
                issue(score_fn, n, step, "diag", diag_buf)
        else:
            overlap(score_fn, step, "diag", diag_buf, c_absorb, buf_absorb)

    def start_next(c_absorb=None, buf_absorb=None):
        start_step(jnp.minimum(g + 1, n_steps - 1), qnext_ref, 1 - slot, c_absorb, buf_absorb)

    @pl.when(g == 0)
    def _():
        start_step(g, q_ref, slot)

    m_ref[...] = jnp.full(m_ref.shape, NEG_INF, F32)
    acc_ref[...] = jnp.zeros(acc_ref.shape, F32)
    score_fn = scorer(g, q_ref, slot)

    @pl.when(g == 0)
    def _():
        for n in range(len(STREAMS)):
            absorb(n, g, diag_buf, diag=True)
        start_next()

    @pl.when(g > 0)
    def _():
        overlap(score_fn, g - 1, "near", 1, g, diag_buf, diag=True)
        n_far = g - 1

        def pair(n, carry):
            c = g - 2 - 2 * n
            overlap(score_fn, c, "far", 0, c + 1, 1)
            overlap(score_fn, c - 1, "far", 1, c, 0)
            return carry

        lax.fori_loop(0, n_far // 2, pair, 0)

        @pl.when(n_far % 2 == 1)
        def _():
            overlap(score_fn, 0, "far", 0, 1, 1)
            start_next(0, 0)

        @pl.when(n_far % 2 == 0)
        def _():
            start_next(0, 1)

    for tile in range(BLOCKS_PER_CHUNK):
        outs = [acc_ref[n, :HEAD_DIM] / acc_ref[n, HEAD_DIM:HEAD_DIM + 1]
                for n, (_, stream_tile) in enumerate(STREAMS) if stream_tile == tile]
        o_ref[0, tile * ATTN_TILE:(tile + 1) * ATTN_TILE, :] = (
            jnp.concatenate(outs, axis=0).T.astype(o_ref.dtype))


def _fox_body(q_ref, qnext_ref, k_ref, cp_ref, vt_ref, o_ref, mask_ref, *flash_refs):
    g = pl.program_id(2)

    @pl.when(g == 0)
    def _():
        r = lax.broadcasted_iota(jnp.int32, (KV_CHUNK, ATTN_TILE), 0)
        t = lax.broadcasted_iota(jnp.int32, (KV_CHUNK, ATTN_TILE), 1)
        for tile in range(BLOCKS_PER_CHUNK):
            mask_ref[tile] = jnp.where(r <= t + tile * ATTN_TILE, 0.0, NEG_INF)

    def scorer(step, q_rows_ref, slot):
        lane = lax.broadcasted_iota(jnp.int32, (ATTN_TILE, LANES), 1)
        qxs = []
        for h, tile in STREAMS:
            q = q_rows_ref[0, tile * ATTN_TILE:(tile + 1) * ATTN_TILE, :]
            on_terms = (lane >= h * DECAY_PARTS) & (lane < (h + 1) * DECAY_PARTS)
            minus_one = jnp.where(on_terms, -1.0, 0.0).astype(BF16)
            qxs.append(jnp.concatenate([_head_only(q, h), minus_one], axis=1))

        def score_fn(n, c, kind):
            h, tile = STREAMS[n]
            rows = _chunk_rows(c)
            kx = jnp.concatenate([k_ref[0, rows, _tile_lanes(h)], cp_ref[0, 0, rows, :]], axis=1)
            if kind == "diag":
                keys = _diag_keys(tile)
                return _dot_nt(kx[:keys], qxs[n]) + mask_ref[tile, :keys]
            return _dot_nt(kx, qxs[n])

        return score_fn

    _flash_step(g, pl.num_programs(2), lambda step, q_rows_ref, slot: None, scorer, q_ref,
                qnext_ref, vt_ref, o_ref, *flash_refs)


def _moba_body(q_ref, qnext_ref, k_ref, vt_ref, bias_ref, o_ref, kbar_ref, kterms_ref, sel_ref,
               far_ref, *flash_refs):
    g = pl.program_id(2)
    n_blk = kbar_ref.shape[0]

    @pl.when(g == 0)
    def _():
        for n in range(n_blk):
            kb = k_ref[0, n * MOBA_BLOCK:(n + 1) * MOBA_BLOCK, :].astype(F32)
            kbar_ref[n:n + 1, :] = jnp.mean(kb, axis=0, keepdims=True)
        rest = kbar_ref[...]
        lane = lax.broadcasted_iota(jnp.int32, (n_blk, LANES), 1)
        for term in range(GATE_TERMS):
            part = rest.astype(BF16)
            rest = rest - part.astype(F32)
            for tile in range(STEP_LANES // LANES):
                part_tile = part[:, tile * LANES:(tile + 1) * LANES]
                for h in range(HEADS_PER_TILE):
                    row = (h * GATE_TERMS + term) * n_blk
                    kterms_ref[tile, row:row + n_blk, :] = jnp.where(
                        lane // HEAD_DIM == h, part_tile, jnp.zeros_like(part_tile))

    def q_tile(q_rows_ref, tile):
        return q_rows_ref[0, tile * ATTN_TILE:(tile + 1) * ATTN_TILE, :]

    def prepare(step, q_rows_ref, slot):
        blk = lax.broadcasted_iota(jnp.int32, (n_blk, ATTN_TILE), 0)
        tile_gates = {}
        for n, (h, tile) in enumerate(STREAMS):
            t = step * BLOCKS_PER_CHUNK + tile
            lanes = h // HEADS_PER_TILE
            if (lanes, tile) not in tile_gates:
                tile_gates[lanes, tile] = _dot_nt(kterms_ref[lanes],
                                                  _lane_tile(q_tile(q_rows_ref, tile), h))
            row = (h % HEADS_PER_TILE) * GATE_TERMS * n_blk
            gates = tile_gates[lanes, tile]
            gate = gates[row:row + n_blk]
            for term in range(1, GATE_TERMS):
                gate = gate + gates[row + term * n_blk:row + (term + 1) * n_blk]
            gate = jnp.where(blk < t, gate, NEG_INF)
            keep = blk < 0
            for _ in range(MOBA_TOPK):
                top = jnp.max(gate, axis=0, keepdims=True)
                first = jnp.min(jnp.where(gate == top, blk, n_blk), axis=0, keepdims=True)
                hit = blk == first
                keep = keep | (hit & (blk < t))
                gate = jnp.where(hit, -jnp.inf, gate)
            sel = jnp.where(keep, 0.0, NEG_INF)
            sel_ref[slot, n] = sel
            far_ref[slot, n] = sel + bias_ref[h, N_BIAS_SLOTS - 1, 0:1, 0:1]

    def scorer(step, q_rows_ref, slot):
        qms = [_head_only(q_tile(q_rows_ref, tile), h) for h, tile in STREAMS]

        def score_fn(n, c, kind):
            h, tile = STREAMS[n]
            chunks_back = {"diag": 0, "near": 1, "far": 2}[kind]
            keys = _diag_keys(tile) if kind == "diag" else KV_CHUNK
            s = _dot_nt(k_ref[0, _chunk_rows(c), _tile_lanes(h)][:keys], qms[n])
            parts = []
            for b in range(keys // ATTN_TILE):
                j = c * BLOCKS_PER_CHUNK + b
                scores = s[b * ATTN_TILE:(b + 1) * ATTN_TILE]
                back = chunks_back * BLOCKS_PER_CHUNK + tile - b
                if back == 0:
                    parts.append(scores + bias_ref[h, 0])
                elif back == 1:
                    parts.append(scores + bias_ref[h, 1] + sel_ref[slot, n, pl.ds(j, 1), :])
                else:
                    parts.append(scores + far_ref[slot, n, pl.ds(j, 1), :])
            return jnp.concatenate(parts, axis=0)

        return score_fn

    _flash_step(g, pl.num_programs(2), prepare, scorer, q_ref, qnext_ref, vt_ref, o_ref,
                *flash_refs)


def _attn_specs(bsz, s):
    n_blk = s // KV_CHUNK
    q_spec = pl.BlockSpec((1, KV_CHUNK, STEP_LANES), lambda b, p, i: (b, i, p))
    once = pl.Buffered(1)
    k_spec = pl.BlockSpec((1, s, STEP_LANES), lambda b, p, i: (b, 0, p), pipeline_mode=once)
    vt_spec = pl.BlockSpec((1, s // KV_CHUNK, STEP_LANES, KV_CHUNK), lambda b, p, i: (b, 0, p, 0),
                           pipeline_mode=once)
    out_shape = jax.ShapeDtypeStruct((bsz, s, W_HEADS), BF16)
    grid = (bsz, W_HEADS // STEP_LANES, n_blk)
    return grid, q_spec, k_spec, vt_spec, out_shape


def _fox_attention(q, k, decay_terms, vt):
    bsz, s, _ = q.shape
    grid, q_spec, k_spec, vt_spec, out_shape = _attn_specs(bsz, s)
    return pl.pallas_call(
        _fox_body,
        grid=grid,
        in_specs=[q_spec, _next_step_spec(grid[2]), k_spec,
                  pl.BlockSpec((1, 1, s, LANES), lambda b, p, i: (b, p, 0, 0),
                               pipeline_mode=pl.Buffered(1)),
                  vt_spec],
        out_specs=q_spec,
        out_shape=out_shape,
        scratch_shapes=[pltpu.VMEM((BLOCKS_PER_CHUNK, KV_CHUNK, ATTN_TILE), F32)]
        + _flash_scratch(),
        compiler_params=_params("parallel", "parallel", "arbitrary"),
        name="fox_attention",
    )(q, q, k, decay_terms, vt)


def _moba_attention(q, k, vt, bias):
    bsz, s, _ = q.shape
    grid, q_spec, k_spec, vt_spec, out_shape = _attn_specs(bsz, s)
    n_blk = s // MOBA_BLOCK
    return pl.pallas_call(
        _moba_body,
        grid=grid,
        in_specs=[q_spec, _next_step_spec(grid[2]), k_spec, vt_spec,
                  pl.BlockSpec((HEADS_PER_STEP, N_BIAS_SLOTS, MOBA_BLOCK, MOBA_BLOCK),
                               lambda b, p, i: (p, 0, 0, 0), pipeline_mode=pl.Buffered(1))],
        out_specs=q_spec,
        out_shape=out_shape,
        scratch_shapes=[pltpu.VMEM((n_blk, STEP_LANES), F32),
                        pltpu.VMEM((STEP_LANES // LANES, HEADS_PER_TILE * GATE_TERMS * n_blk, LANES),
                                   BF16),
                        pltpu.VMEM((2, len(STREAMS), n_blk, ATTN_TILE), F32),
                        pltpu.VMEM((2, len(STREAMS), n_blk, ATTN_TILE), F32)] + _flash_scratch(),
        compiler_params=_params("parallel", "parallel", "arbitrary"),
        name="moba_attention",
    )(q, q, k, vt, bias)


def _gelu_tanh(x):
    c = 2.0 * math.sqrt(2.0 / math.pi)
    return x * jax.nn.sigmoid(x * (c + (c * 0.044715) * (x * x)))


def _sgu_body(x_ref, win_ref, bin_ref, lg_ref, lb_ref, ws_ref, bs_ref, wout_ref, g_ref, b_ref,
              o_ref, y_ref):
    x = x_ref[...]
    xb = x.astype(BF16)
    w = x.shape[1]
    u = _gelu_tanh(_dot(xb, win_ref[:, :w]) + bin_ref[:, :w])
    v = _gelu_tanh(_dot(xb, win_ref[:, w:]) + bin_ref[:, w:])
    vb = _layer_norm(v, lg_ref[...], lb_ref[...]).astype(BF16)
    t = lax.broadcasted_iota(jnp.int32, (SGU_CHUNK, SGU_CHUNK), 0)
    s = lax.broadcasted_iota(jnp.int32, (SGU_CHUNK, SGU_CHUNK), 1)
    gw = w // SGU_GROUPS
    for g in range(SGU_GROUPS):
        w_g = jnp.where(t >= s, ws_ref[g], 0.0).astype(BF16)
        cols = slice(g * gw, (g + 1) * gw)
        for c in range(x.shape[0] // SGU_CHUNK):
            rows = slice(c * SGU_CHUNK, (c + 1) * SGU_CHUNK)
            mixed = _dot(w_g, vb[rows, cols]) + bs_ref[:, g:g + 1]
            y_ref[rows, cols] = (u[rows, cols] * mixed).astype(BF16)
    out = _dot(y_ref[...], wout_ref[...])
    o_ref[...] = _layer_norm(DN_ALPHA * x + out, g_ref[...], b_ref[...])


def _sgu(x, w_in, b_in, ln_g, ln_b, w_s, b_s_t, w_out, g, b):
    m, d = x.shape
    tm = min(ROW_TILE, m)
    row = pl.BlockSpec((tm, d), lambda i: (i, 0))
    weights = [w_in, b_in, ln_g, ln_b, w_s, b_s_t, w_out, g, b]
    return pl.pallas_call(
        _sgu_body,
        grid=(m // tm,),
        in_specs=[row] + [_resident(a.shape) for a in weights],
        out_specs=row,
        out_shape=jax.ShapeDtypeStruct((m, d), F32),
        scratch_shapes=[pltpu.VMEM((tm, d), BF16)],
        compiler_params=_params("parallel"),
        name="sgu",
    )(x, *weights)


def _mixer_attention(x2, bsz, s, w_in, b_f, rel_bias):
    d = x2.shape[1]
    seg = lambda n: w_in[:, n * W_HEADS:(n + 1) * W_HEADS]
    w_rows = jnp.concatenate([seg(0), seg(1), seg(3), seg(4)], axis=1).astype(BF16)
    w_t = jnp.concatenate([seg(2), seg(5), w_in[:, 6 * W_HEADS:]], axis=1).T.astype(BF16)
    qa, ka, qb, kb, vta, vtb, f_t = _attn_proj(x2.reshape(bsz, s, d), w_rows, w_t)
    n_groups = N_HEADS_B // HEADS_PER_STEP
    terms = _fox_decay(f_t, b_f).reshape(bsz, DECAY_PARTS, n_groups, HEADS_PER_STEP, s)
    terms = terms.transpose(0, 2, 4, 3, 1).reshape(bsz, n_groups, s, HEADS_PER_STEP * DECAY_PARTS)
    terms = jnp.pad(terms.astype(BF16), ((0, 0),) * 3 + ((0, LANES - terms.shape[-1]),))
    o_a = _moba_attention(qa, ka, vta, _moba_bias(rel_bias))
    o_b = _fox_attention(qb, kb, terms, vtb)
    return o_a.reshape(bsz * s, W_HEADS), o_b.reshape(bsz * s, W_HEADS)


def kernel(x, p, ln_g, ln_b, ffn_w_gate, ffn_w_up, ffn_w_down, attn_w_in, attn_b_f, attn_w_out,
           rel_bias, sgu_w_in, sgu_b_in, sgu_ln_g, sgu_ln_b, sgu_w_s, sgu_b_s, sgu_w_out,
           ple_w_proj, ple_w_gate):
    bsz, s, d = x.shape
    x2 = x.reshape(bsz * s, d)
    vec = lambda a: a.reshape(1, -1)
    wg, wu, wd = ffn_w_gate.astype(BF16), ffn_w_up.astype(BF16), ffn_w_down.astype(BF16)
    ln_g4, ln_b4 = ln_g.reshape(DEPTH, 3, 1, d), ln_b.reshape(DEPTH, 3, 1, d)
    p3 = p.reshape(DEPTH, bsz * s, p.shape[-1])
    w_ple_gate, w_ple_proj = ple_w_gate.astype(BF16), ple_w_proj.astype(BF16)
    w_attn_out = attn_w_out.astype(BF16).reshape(attn_w_out.shape[0], 2, W_HEADS, d)
    for i in range(DEPTH):
        j = i // 2
        ln = lambda n: [_pick(ln_g4, i, n), _pick(ln_b4, i, n)]
        ffn_w = lambda n: [_pick(wg, i, n), _pick(wu, i, n), _pick(wd, i, n)] + ln(2 * n)
        x2 = _ffn(x2, ffn_w(0))
        ple = ((p3, i), _pick(w_ple_gate, i), _pick(w_ple_proj, i))
        if i % 2 == 0:
            o_a, o_b = _mixer_attention(x2, bsz, s, attn_w_in[j], attn_b_f[j], rel_bias)
            mix = [o_a, o_b, _pick(w_attn_out, j, 0), _pick(w_attn_out, j, 1)] + ln(1)
            x2 = _ffn(x2, ffn_w(1), mix=mix, ple=ple)
        else:
            x2 = _sgu(x2, sgu_w_in[j].astype(BF16), vec(sgu_b_in[j]), vec(sgu_ln_g[j]),
                      vec(sgu_ln_b[j]), sgu_w_s[j], sgu_b_s[j].T, sgu_w_out[j].astype(BF16),
                      vec(ln_g[i, 1]), vec(ln_b[i, 1]))
            x2 = _ffn(x2, ffn_w(1), ple=ple)
    return x2.reshape(bsz, s, d)
```
